```python
import jax
import jax.numpy as jnp
from jax import lax
import numpy as np

D_MODEL = 2048
BATCH = 4
SEQ = 4096
DEPTH = 2

GROUP_WIDTH = D_MODEL // 4
GDN_HEAD_DIM = 128
GDN_HEADS = GROUP_WIDTH // GDN_HEAD_DIM
HGRN_HEAD_DIM = 128
HGRN_HEADS = GROUP_WIDTH // HGRN_HEAD_DIM
S5_CH = 16
S5_GROUPS = GROUP_WIDTH // S5_CH
S5_STATE = 64
LRU_BLOCKS = 8
LRU_BLOCK_DIM = GROUP_WIDTH // LRU_BLOCKS
LRU_C = 8.0
CONV_WIDTH = 4
CHUNK = 64
D_FF = ((8 * D_MODEL // 3 + 255) // 256) * 256
PLE_DIM = 256
DN_ALPHA = (2 * DEPTH) ** 0.25
DN_BETA = (8 * DEPTH) ** -0.25
LN_EPS = 1e-5
RMS_EPS = 1e-6
IN_SPLIT = (3 * GROUP_WIDTH, GROUP_WIDTH, GDN_HEADS, GDN_HEADS,
            GROUP_WIDTH, GROUP_WIDTH, GROUP_WIDTH, GROUP_WIDTH,
            GROUP_WIDTH,
            GROUP_WIDTH, GROUP_WIDTH)
IN_COLS = sum(IN_SPLIT)

kernel_name = 'hybrid_parallel_heads_gdn_hgrn2_s5_rglru'


def layer_norm(x, g, b):
    xf = x.astype(jnp.float32)
    mu = jnp.mean(xf, axis=-1, keepdims=True)
    xc = xf - mu
    var = jnp.mean(xc * xc, axis=-1, keepdims=True)
    y = xc * lax.rsqrt(var + LN_EPS) * g.astype(jnp.float32) + b.astype(jnp.float32)
    return y.astype(x.dtype)


def rms_norm(x, g):
    xf = x.astype(jnp.float32)
    y = xf * lax.rsqrt(jnp.mean(xf * xf, axis=-1, keepdims=True) + RMS_EPS) * g.astype(jnp.float32)
    return y.astype(x.dtype)


def l2norm(t):
    return t * lax.rsqrt(jnp.sum(t * t, axis=-1, keepdims=True) + RMS_EPS)


def swiglu(x, wi, wo):
    gate, up = jnp.split(x @ wi, 2, axis=-1)
    return (jax.nn.silu(gate) * up) @ wo


def causal_conv(x, w):
    k, c = w.shape
    return lax.conv_general_dilated(x, w[:, None, :].astype(x.dtype), window_strides=(1,),
                                    padding=((k - 1, 0),), dimension_numbers=('NWC', 'WIO', 'NWC'),
                                    feature_group_count=c)


def to_chunks(t):
    bsz, t_len, h, d = t.shape
    return t.reshape(bsz, t_len // CHUNK, CHUNK, h, d).transpose(1, 0, 3, 2, 4)


def from_chunks(t):
    nc, bsz, h, c, d = t.shape
    return t.transpose(1, 0, 3, 2, 4).reshape(bsz, nc * c, h, d)


def gated_deltanet(q, k, v, z, beta_logit, decay_logit, A_log, dt_bias, norm_g):
    f32 = jnp.float32
    dk = q.shape[-1]
    dv = v.shape[-1]
    q = l2norm(q.astype(f32)) * dk ** -0.5
    k = l2norm(k.astype(f32))
    v = v.astype(f32)
    beta = jax.nn.sigmoid(beta_logit.astype(f32))
    g = -jnp.exp(A_log.astype(f32)) * jax.nn.softplus(decay_logit.astype(f32) + dt_bias.astype(f32))
    qc, kc, vc = to_chunks(q), to_chunks(k), to_chunks(v)
    beta_c = to_chunks(beta[..., None])[..., 0]
    G = jnp.cumsum(to_chunks(g[..., None])[..., 0], axis=-1)
    idx = jnp.arange(CHUNK)
    incl = idx[:, None] >= idx[None, :]
    strict = idx[:, None] > idx[None, :]
    diff = G[..., :, None] - G[..., None, :]
    decay_incl = jnp.exp(jnp.where(incl, diff, -jnp.inf))
    decay_strict = jnp.where(strict, decay_incl, 0.0)
    a_mat = beta_c[..., :, None] * jnp.einsum('nbhtd,nbhsd->nbhts', kc, kc) * decay_strict
    rhs = jnp.concatenate([beta_c[..., None] * vc, (beta_c * jnp.exp(G))[..., None] * kc], axis=-1)
    sol = lax.linalg.triangular_solve(a_mat, rhs, left_side=True, lower=True, unit_diagonal=True)
    u, w = sol[..., :dv], sol[..., dv:]
    qk = jnp.einsum('nbhtd,nbhsd->nbhts', qc, kc) * decay_incl
    q_g = qc * jnp.exp(G)[..., None]
    k_d = kc * jnp.exp(G[..., -1:] - G)[..., None]
    g_last = jnp.exp(G[..., -1])

    def step(S, inp):
        qg_c, qk_c, u_c, w_c, kd_c, gl_c = inp
        u_hat = u_c - jnp.einsum('bhcd,bhde->bhce', w_c, S)
        o = jnp.einsum('bhcd,bhde->bhce', qg_c, S) + jnp.einsum('bhts,bhse->bhte', qk_c, u_hat)
        S = gl_c[..., None, None] * S + jnp.einsum('bhcd,bhce->bhde', kd_c, u_hat)
        return S, o

    S0 = jnp.zeros(qc.shape[1:3] + (dk, dv), f32)
    _, o = lax.scan(step, S0, (q_g, qk, u, w, k_d, g_last))
    o = rms_norm(from_chunks(o), norm_g) * jax.nn.silu(z.astype(f32))
    return o.reshape(o.shape[0], o.shape[1], -1)


def hgrn2(q, f_logit, i_in, g, lb, norm_g):
    f32 = jnp.float32
    dk = q.shape[-1]
    q = q.astype(f32) * dk ** -0.5
    lb = lb.astype(f32)
    log_f = jnp.logaddexp(jnp.log(lb), jnp.log1p(-lb) + jax.nn.log_sigmoid(f_logit.astype(f32)))
    k = -jnp.expm1(log_f)
    qc, kc, vc = to_chunks(q), to_chunks(k), to_chunks(i_in.astype(f32))
    bc = jnp.cumsum(to_chunks(log_f), axis=3)
    idx = jnp.arange(CHUNK)
    incl = (idx[:, None] >= idx[None, :])[:, :, None]

    def step(S, inp):
        q_c, k_c, v_c, b_c = inp
        diff = b_c[:, :, :, None, :] - b_c[:, :, None, :, :]
        decay = jnp.exp(jnp.where(incl, diff, -jnp.inf))
        att = jnp.einsum('bhtd,bhsd,bhtsd->bhts', q_c, k_c, decay)
        b_last = b_c[:, :, -1:, :]
        o = jnp.einsum('bhtd,bhde->bhte', q_c * jnp.exp(b_c), S) + jnp.einsum('bhts,bhse->bhte', att, v_c)
        S = jnp.exp(b_last[:, :, 0, :])[..., None] * S + jnp.einsum('bhsd,bhse->bhde', k_c * jnp.exp(b_last - b_c), v_c)
        return S, o

    S0 = jnp.zeros(qc.shape[1:3] + (dk, vc.shape[-1]), f32)
    _, o = lax.scan(step, S0, (qc, kc, vc, bc))
    o = rms_norm(from_chunks(o), norm_g) * jax.nn.silu(g.astype(f32))
    return o.reshape(o.shape[0], o.shape[1], -1)


def complex_combine(left, right):
    ar_l, ai_l, br_l, bi_l = left
    ar_r, ai_r, br_r, bi_r = right
    return (ar_r * ar_l - ai_r * ai_l,
            ar_r * ai_l + ai_r * ar_l,
            ar_r * br_l - ai_r * bi_l + br_r,
            ar_r * bi_l + ai_r * br_l + bi_r)


def real_combine(left, right):
    return (right[0] * left[0], right[0] * left[1] + right[1])


def s5(u, lam_re, lam_im, log_dt, B_re, B_im, C_re, C_im, D, glu_w, glu_b):
    f32 = jnp.float32
    bsz, t_len, width = u.shape
    uf = u.astype(f32).reshape(bsz, t_len, S5_GROUPS, S5_CH)
    lam_re = lam_re.astype(f32)
    lam_im = lam_im.astype(f32)
    dt = jnp.exp(log_dt.astype(f32))[:, None]
    mag = jnp.exp(lam_re * dt)
    ang = lam_im * dt
    ab_re, ab_im = mag * jnp.cos(ang), mag * jnp.sin(ang)
    den = lam_re * lam_re + lam_im * lam_im
    nr = ab_re - 1.0
    c_re = (nr * lam_re + ab_im * lam_im) / den
    c_im = (ab_im * lam_re - nr * lam_im) / den
    B_re = B_re.astype(f32)
    B_im = B_im.astype(f32)
    bb_re = c_re[..., None] * B_re - c_im[..., None] * B_im
    bb_im = c_re[..., None] * B_im + c_im[..., None] * B_re
    bu_re = jnp.einsum('btgp,gnp->btgn', uf, bb_re)
    bu_im = jnp.einsum('btgp,gnp->btgn', uf, bb_im)
    a_shape = (1, t_len) + ab_re.shape
    a_re = jnp.broadcast_to(ab_re, a_shape)
    a_im = jnp.broadcast_to(ab_im, a_shape)
    _, _, x_re, x_im = lax.associative_scan(complex_combine, (a_re, a_im, bu_re, bu_im), axis=1)
    y = (jnp.einsum('btgn,gpn->btgp', x_re, C_re.astype(f32))
         - jnp.einsum('btgn,gpn->btgp', x_im, C_im.astype(f32))
         + D.astype(f32).reshape(S5_GROUPS, S5_CH) * uf)
    y = jax.nn.gelu(y.reshape(bsz, t_len, width))
    return y * jax.nn.sigmoid(y @ glu_w.astype(f32) + glu_b.astype(f32))


def rglru(xb, gb, conv_w, conv_b, wa, ba, wx, bx, lru_param):
    f32 = jnp.float32
    bsz, t_len, width = xb.shape
    xc = (causal_conv(xb, conv_w) + conv_b).astype(f32)
    xh = xc.reshape(bsz, t_len, LRU_BLOCKS, LRU_BLOCK_DIM)
    r = jax.nn.sigmoid(jnp.einsum('btnd,nde->btne', xh, wa.astype(f32)).reshape(bsz, t_len, width) + ba.astype(f32))
    gi = jax.nn.sigmoid(jnp.einsum('btnd,nde->btne', xh, wx.astype(f32)).reshape(bsz, t_len, width) + bx.astype(f32))
    log_a = -LRU_C * r * jax.nn.softplus(-lru_param.astype(f32))
    a = jnp.exp(log_a)
    mult = jnp.sqrt(-jnp.expm1(2.0 * log_a))
    _, h = lax.associative_scan(real_combine, (a, mult * gi * xc), axis=1)
    return h * jax.nn.gelu(gb.astype(f32))


def token_mix(h, w_in, w_out, gdn_conv_w, gdn_A_log, gdn_dt_bias, gdn_norm_g, hgrn_lb, hgrn_norm_g,
              s5_lam_re, s5_lam_im, s5_log_dt, s5_B_re, s5_B_im, s5_C_re, s5_C_im, s5_D, s5_glu_w, s5_glu_b,
              lru_conv_w, lru_conv_b, lru_wa, lru_ba, lru_wx, lru_bx, lru_param, branch_norm_g):
    bsz, t_len, _ = h.shape
    cuts = [int(c) for c in np.cumsum(IN_SPLIT)[:-1]]
    (a_qkv, a_z, a_beta, a_dec, b_q, b_f, b_i, b_g, c_u, d_x, d_g) = jnp.split(h @ w_in, cuts, axis=-1)

    def heads(t, d):
        return t.reshape(bsz, t_len, -1, d)

    qkv = jax.nn.silu(causal_conv(a_qkv, gdn_conv_w))
    a_q, a_k, a_v = jnp.split(qkv, 3, axis=-1)
    y_a = gated_deltanet(heads(a_q, GDN_HEAD_DIM), heads(a_k, GDN_HEAD_DIM), heads(a_v, GDN_HEAD_DIM),
                         heads(a_z, GDN_HEAD_DIM), a_beta, a_dec, gdn_A_log, gdn_dt_bias, gdn_norm_g)
    y_b = hgrn2(heads(b_q, HGRN_HEAD_DIM), heads(b_f, HGRN_HEAD_DIM), heads(b_i, HGRN_HEAD_DIM),
                heads(b_g, HGRN_HEAD_DIM), hgrn_lb, hgrn_norm_g)
    y_c = rms_norm(s5(c_u, s5_lam_re, s5_lam_im, s5_log_dt, s5_B_re, s5_B_im, s5_C_re, s5_C_im, s5_D,
                      s5_glu_w, s5_glu_b), branch_norm_g[0])
    y_d = rms_norm(rglru(d_x, d_g, lru_conv_w, lru_conv_b, lru_wa, lru_ba, lru_wx, lru_bx, lru_param),
                   branch_norm_g[1])
    y = jnp.concatenate([y_a, y_b, y_c, y_d], axis=-1).astype(h.dtype)
    return y @ w_out


def setup_inputs(seed: int = 0) -> dict:
    key = jax.random.key(seed)
    ks = jax.random.split(key, 40)
    f32 = jnp.float32

    def nrm(k, shape, scale):
        return jax.random.normal(k, shape, f32) * scale

    def unif(k, shape, lo, hi):
        return jax.random.uniform(k, shape, f32, lo, hi)

    W = GROUP_WIDTH
    dt = jnp.exp(unif(ks[9], (DEPTH, GDN_HEADS), np.log(1e-3), np.log(1e-1)))
    a0 = unif(ks[30], (DEPTH, W), 0.9, 0.999)
    sig = a0 ** (1.0 / LRU_C)
    n_idx = jnp.arange(S5_STATE, dtype=f32)
    return {
        'x': nrm(ks[0], (BATCH, SEQ, D_MODEL), 1.0),
        'p': nrm(ks[1], (DEPTH, BATCH, SEQ, PLE_DIM), 1.0),
        'ln_g': 1.0 + nrm(ks[2], (DEPTH, 4, D_MODEL), 0.02),
        'ln_b': nrm(ks[3], (DEPTH, 4, D_MODEL), 0.01),
        'ffn_wi': nrm(ks[4], (DEPTH, 2, D_MODEL, 2 * D_FF), D_MODEL ** -0.5),
        'ffn_wo': nrm(ks[5], (DEPTH, 2, D_FF, D_MODEL), D_FF ** -0.5 * DN_BETA),
        'mix_w_in': nrm(ks[6], (DEPTH, D_MODEL, IN_COLS), D_MODEL ** -0.5),
        'mix_w_out': nrm(ks[7], (DEPTH, D_MODEL, D_MODEL), D_MODEL ** -0.5 * DN_BETA),
        'gdn_conv_w': nrm(ks[8], (DEPTH, CONV_WIDTH, 3 * W), 0.5),
        'gdn_A_log': jnp.log(unif(ks[10], (DEPTH, GDN_HEADS), 1.0, 16.0)),
        'gdn_dt_bias': dt + jnp.log(-jnp.expm1(-dt)),
        'gdn_norm_g': 1.0 + nrm(ks[11], (DEPTH, GDN_HEAD_DIM), 0.02),
        'hgrn_lb_logits': nrm(ks[12], (DEPTH, W), 0.5),
        'hgrn_norm_g': 1.0 + nrm(ks[13], (DEPTH, HGRN_HEAD_DIM), 0.02),
        's5_lam_re': -0.5 + nrm(ks[14], (DEPTH, S5_GROUPS, S5_STATE), 0.01),
        's5_lam_im': np.pi * n_idx + nrm(ks[15], (DEPTH, S5_GROUPS, S5_STATE), 0.01),
        's5_log_dt': unif(ks[16], (DEPTH, S5_GROUPS), np.log(1e-3), np.log(1e-1)),
        's5_B_re': nrm(ks[17], (DEPTH, S5_GROUPS, S5_STATE, S5_CH), (2 * S5_CH) ** -0.5),
        's5_B_im': nrm(ks[18], (DEPTH, S5_GROUPS, S5_STATE, S5_CH), (2 * S5_CH) ** -0.5),
        's5_C_re': nrm(ks[19], (DEPTH, S5_GROUPS, S5_CH, S5_STATE), (2 * S5_STATE) ** -0.5),
        's5_C_im': nrm(ks[20], (DEPTH, S5_GROUPS, S5_CH, S5_STATE), (2 * S5_STATE) ** -0.5),
        's5_D': nrm(ks[21], (DEPTH, W), 1.0),
        's5_glu_w': nrm(ks[22], (DEPTH, W, W), W ** -0.5),
        's5_glu_b': nrm(ks[23], (DEPTH, W), 0.01),
        'lru_conv_w': nrm(ks[24], (DEPTH, CONV_WIDTH, W), 0.5),
        'lru_conv_b': nrm(ks[25], (DEPTH, W), 0.01),
        'lru_wa': nrm(ks[26], (DEPTH, LRU_BLOCKS, LRU_BLOCK_DIM, LRU_BLOCK_DIM), LRU_BLOCK_DIM ** -0.5),
        'lru_ba': nrm(ks[27], (DEPTH, W), 0.01),
        'lru_wx': nrm(ks[28], (DEPTH, LRU_BLOCKS, LRU_BLOCK_DIM, LRU_BLOCK_DIM), LRU_BLOCK_DIM ** -0.5),
        'lru_bx': nrm(ks[29], (DEPTH, W), 0.01),
        'lru_param': jnp.log(sig) - jnp.log1p(-sig),
        'branch_norm_g': 1.0 + nrm(ks[31], (DEPTH, 2, W), 0.02),
        'ple_w': nrm(ks[32], (DEPTH, PLE_DIM, D_MODEL), PLE_DIM ** -0.5 * DN_BETA),
        'ple_gate_w': nrm(ks[33], (DEPTH, D_MODEL, D_MODEL), D_MODEL ** -0.5),
    }


def reference(x, p, ln_g, ln_b, ffn_wi, ffn_wo, mix_w_in, mix_w_out, gdn_conv_w, gdn_A_log, gdn_dt_bias,
              gdn_norm_g, hgrn_lb_logits, hgrn_norm_g, s5_lam_re, s5_lam_im, s5_log_dt, s5_B_re, s5_B_im,
              s5_C_re, s5_C_im, s5_D, s5_glu_w, s5_glu_b, lru_conv_w, lru_conv_b, lru_wa, lru_ba, lru_wx,
              lru_bx, lru_param, branch_norm_g, ple_w, ple_gate_w):
    lb_cum = jnp.cumsum(jax.nn.softmax(hgrn_lb_logits.astype(jnp.float32), axis=0), axis=0)
    lower_bounds = lb_cum - lb_cum[0:1]
    for i in range(DEPTH):
        x = layer_norm(DN_ALPHA * x + 0.5 * swiglu(x, ffn_wi[i, 0], ffn_wo[i, 0]), ln_g[i, 0], ln_b[i, 0])
        mix = token_mix(x, mix_w_in[i], mix_w_out[i], gdn_conv_w[i], gdn_A_log[i], gdn_dt_bias[i], gdn_norm_g[i],
                        lower_bounds[i].reshape(HGRN_HEADS, HGRN_HEAD_DIM), hgrn_norm_g[i],
                        s5_lam_re[i], s5_lam_im[i], s5_log_dt[i], s5_B_re[i], s5_B_im[i], s5_C_re[i], s5_C_im[i],
                        s5_D[i], s5_glu_w[i], s5_glu_b[i], lru_conv_w[i], lru_conv_b[i], lru_wa[i], lru_ba[i],
                        lru_wx[i], lru_bx[i], lru_param[i], branch_norm_g[i])
        x = layer_norm(DN_ALPHA * x + mix, ln_g[i, 1], ln_b[i, 1])
        x = layer_norm(DN_ALPHA * x + 0.5 * swiglu(x, ffn_wi[i, 1], ffn_wo[i, 1]), ln_g[i, 2], ln_b[i, 2])
        ple = (p[i] @ ple_w[i]) * jax.nn.sigmoid(x @ ple_gate_w[i])
        x = layer_norm(DN_ALPHA * x + ple, ln_g[i, 3], ln_b[i, 3])
    return x
```

```python
import functools

import numpy as np
import jax
import jax.numpy as jnp
from jax import lax
from jax.experimental import pallas as pl
from jax.experimental.pallas import tpu as pltpu

F32 = jnp.float32
BF16 = jnp.bfloat16
HI = lax.Precision.HIGHEST

D_MODEL = 2048
DEPTH = 2
GROUP_WIDTH = D_MODEL // 4
HEAD_DIM = 128
N_HEADS = GROUP_WIDTH // HEAD_DIM
S5_CH = 16
S5_GROUPS = GROUP_WIDTH // S5_CH
S5_STATE = 64
LRU_BLOCKS = 8
LRU_C = 8.0
CONV_WIDTH = 4
D_FF = ((8 * D_MODEL // 3 + 255) // 256) * 256
PLE_DIM = 256
DN_ALPHA = (2 * DEPTH) ** 0.25
LN_EPS = 1e-5
RMS_EPS = 1e-6

SUBLANES = 8
LANES = 128
VMEM_LIMIT = 56 * 1024 * 1024

CHUNK = 64
MIX_BLOCK = 256
S5_L = 16
IN_COLS_PAD = 6144
COL_QKV, COL_Z, COL_HG, COL_S5, COL_LX, COL_LG, COL_BD = 0, 1536, 2048, 4096, 4608, 5120, 5632


def _params(sem):
    return pltpu.CompilerParams(dimension_semantics=sem, vmem_limit_bytes=VMEM_LIMIT)


def _sigmoid(x):
    return 1.0 / (1.0 + jnp.exp(-x))


def _silu(x):
    return x * _sigmoid(x)


def _softplus(x):
    return jnp.maximum(x, 0.0) + jnp.log(1.0 + jnp.exp(-jnp.abs(x)))


def _gelu(x):
    return 0.5 * x * (1.0 + jnp.tanh(0.7978845608028654 * (x + 0.044715 * x * x * x)))


def _layer_norm(y, g, b):
    mu = jnp.mean(y, axis=-1, keepdims=True)
    yc = y - mu
    var = jnp.mean(yc * yc, axis=-1, keepdims=True)
    return yc * lax.rsqrt(var + LN_EPS) * g + b


def _rms(y, g):
    return y * lax.rsqrt(jnp.mean(y * y, axis=-1, keepdims=True) + RMS_EPS) * g


def _dot(a, b, precision=None):
    return jnp.dot(a, b, preferred_element_type=F32, precision=precision)


def _dot_nt(a, b, precision=None):
    return lax.dot_general(a, b, (((1,), (1,)), ((), ())), preferred_element_type=F32, precision=precision)


def _dot_tn(a, b, precision=None):
    return lax.dot_general(a, b, (((0,), (0,)), ((), ())), preferred_element_type=F32, precision=precision)


def _ffn_kernel(x_ref, wg_ref, wu_ref, wo_ref, g_ref, b_ref, o_ref, xb_ref, acc_ref):
    f = pl.program_id(1)

    @pl.when(f == 0)
    def _():
        xb_ref[...] = x_ref[...].astype(BF16)
        acc_ref[...] = jnp.zeros_like(acc_ref)

    xb = xb_ref[...]
    hg = _dot(xb, wg_ref[...])
    hu = _dot(xb, wu_ref[...])
    act = (_silu(hg) * hu).astype(BF16)
    acc_ref[...] += _dot(act, wo_ref[...])

    @pl.when(f == pl.num_programs(1) - 1)
    def _():
        y = DN_ALPHA * x_ref[...] + 0.5 * acc_ref[...]
        o_ref[...] = _layer_norm(y, g_ref[...], b_ref[...])


def _ffn(x, wi, wo, g, b, tm=512, tf=512):
    n, d = x.shape
    dff = wo.shape[0]
    tm = min(tm, n)
    nf = dff // tf
    return pl.pallas_call(
        _ffn_kernel,
        grid=(n // tm, nf),
        in_specs=[
            pl.BlockSpec((tm, d), lambda i, f: (i, 0)),
            pl.BlockSpec((d, tf), lambda i, f: (0, f)),
            pl.BlockSpec((d, tf), lambda i, f: (0, f + nf)),
            pl.BlockSpec((tf, d), lambda i, f: (f, 0)),
            pl.BlockSpec((1, d), lambda i, f: (0, 0)),
            pl.BlockSpec((1, d), lambda i, f: (0, 0)),
        ],
        out_specs=pl.BlockSpec((tm, d), lambda i, f: (i, 0)),
        out_shape=jax.ShapeDtypeStruct((n, d), F32),
        scratch_shapes=[pltpu.VMEM((tm, d), BF16), pltpu.VMEM((tm, d), F32)],
        compiler_params=_params(("parallel", "arbitrary")),
        name="ffn",
    )(x, wi, wi, wo, g, b)


def _proj_kernel(x_ref, w_ref, o_ref, xb_ref):
    @pl.when(pl.program_id(1) == 0)
    def _():
        xb_ref[...] = x_ref[...].astype(BF16)

    o_ref[...] = _dot(xb_ref[...], w_ref[...])


def _proj(x, w, tm=1024, tn=1024):
    n, d = x.shape
    cols = w.shape[1]
    tm = min(tm, n)
    return pl.pallas_call(
        _proj_kernel,
        grid=(n // tm, cols // tn),
        in_specs=[pl.BlockSpec((tm, d), lambda i, j: (i, 0)),
                  pl.BlockSpec((d, tn), lambda i, j: (0, j))],
        out_specs=pl.BlockSpec((tm, tn), lambda i, j: (i, j)),
        out_shape=jax.ShapeDtypeStruct((n, cols), F32),
        scratch_shapes=[pltpu.VMEM((tm, d), BF16)],
        compiler_params=_params(("parallel", "arbitrary")),
        name="in_proj",
    )(x, w)


def _causal_conv(x, w_ref, xe_ref, first):
    tb = x.shape[0]

    @pl.when(first)
    def _():
        xe_ref[0:SUBLANES, :] = jnp.zeros((SUBLANES, x.shape[1]), F32)

    xe_ref[SUBLANES:SUBLANES + tb, :] = x
    acc = x * w_ref[CONV_WIDTH - 1:CONV_WIDTH, :]
    for j in range(CONV_WIDTH - 1):
        off = SUBLANES - (CONV_WIDTH - 1) + j
        acc = acc + xe_ref[off:off + tb, :] * w_ref[j:j + 1, :]
    xe_ref[0:SUBLANES, :] = xe_ref[tb:tb + SUBLANES, :]
    return acc


def _inv_unit_lower(a, eye):
    x = eye - a
    p = _dot(a, a, HI)
    n = 2
    while True:
        x = x + _dot(x, p, HI)
        n *= 2
        if n >= a.shape[0]:
            return x
        p = _dot(p, p, HI)


def _gdn_kernel(qkv_ref, z_ref, bd_ref, cw_ref, alog_ref, dtb_ref, ng_ref, tri_ref, o_ref,
                xe_ref, qkv_s, g_s, s_ref):
    tb = qkv_ref.shape[0]
    w = GROUP_WIDTH
    first = pl.program_id(1) == 0

    @pl.when(first)
    def _():
        s_ref[...] = jnp.zeros_like(s_ref)

    qkv = _silu(_causal_conv(qkv_ref[...], cw_ref, xe_ref, first))
    for h in range(N_HEADS):
        for part in range(3):
            lo = part * w + h * HEAD_DIM
            t = qkv[:, lo:lo + HEAD_DIM]
            if part < 2:
                t = t * lax.rsqrt(jnp.sum(t * t, axis=-1, keepdims=True) + RMS_EPS)
                if part == 0:
                    t = t * (HEAD_DIM ** -0.5)
            qkv_s[:, lo:lo + HEAD_DIM] = t
    bd = bd_ref[...]
    g_s[...] = -jnp.exp(alog_ref[...]) * _softplus(bd + dtb_ref[...])

    r = lax.broadcasted_iota(jnp.int32, (CHUNK, CHUNK), 0)
    c = lax.broadcasted_iota(jnp.int32, (CHUNK, CHUNK), 1)
    incl = r >= c
    strict = r > c
    eye = (r == c).astype(F32)
    tri = tri_ref[...]
    ng = ng_ref[...]

    def chunk(ci, carry):
        rows = pl.ds(pl.multiple_of(ci * CHUNK, CHUNK), CHUNK)
        gcum = _dot(tri, g_s[rows, :], HI)
        gcum_t = gcum.T
        beta = _sigmoid(bd_ref[rows, :])
        for h in range(N_HEADS):
            q = qkv_s[rows, h * HEAD_DIM:(h + 1) * HEAD_DIM]
            k = qkv_s[rows, w + h * HEAD_DIM:w + (h + 1) * HEAD_DIM]
            v = qkv_s[rows, 2 * w + h * HEAD_DIM:2 * w + (h + 1) * HEAD_DIM]
            gc = gcum[:, N_HEADS + h:N_HEADS + h + 1]
            gr = gcum_t[N_HEADS + h:N_HEADS + h + 1, :]
            bc = beta[:, h:h + 1]
            dec = jnp.exp(jnp.where(incl, gc - gr, -jnp.inf))
            a = bc * _dot_nt(k, k, HI) * jnp.where(strict, dec, 0.0)
            qk = _dot_nt(q, k, HI) * dec
            eg = jnp.exp(gc)
            rhs = jnp.concatenate([bc * v, (bc * eg) * k], axis=1)
            sol = _dot(_inv_unit_lower(a, eye), rhs, HI)
            u = sol[:, :HEAD_DIM]
            wk = sol[:, HEAD_DIM:]
            glast = gc[CHUNK - 1:CHUNK, :]
            kd = k * jnp.exp(glast - gc)
            s = s_ref[h]
            u_hat = u - _dot(wk, s, HI)
            o = _dot(q * eg, s, HI) + _dot(qk, u_hat, HI)
            s_ref[h] = jnp.exp(glast) * s + _dot_tn(kd, u_hat, HI)
            zz = z_ref[rows, h * HEAD_DIM:(h + 1) * HEAD_DIM]
            o_ref[rows, h * HEAD_DIM:(h + 1) * HEAD_DIM] = _rms(o, ng) * _silu(zz)
        return carry

    lax.fori_loop(0, tb // CHUNK, chunk, 0)


def _gdn(y, bsz, t_len, conv_w, alog_row, dtb_row, norm_g, tri):
    tb = min(MIX_BLOCK, t_len)
    nb = t_len // tb
    w = GROUP_WIDTH
    row = lambda b, i: b * nb + i
    const = lambda b, i: (0, 0)
    return pl.pallas_call(
        _gdn_kernel,
        grid=(bsz, nb),
        in_specs=[
            pl.BlockSpec((tb, 3 * w), lambda b, i: (row(b, i), COL_QKV // (3 * w))),
            pl.BlockSpec((tb, w), lambda b, i: (row(b, i), COL_Z // w)),
            pl.BlockSpec((tb, LANES), lambda b, i: (row(b, i), COL_BD // LANES)),
            pl.BlockSpec((CONV_WIDTH, 3 * w), const),
            pl.BlockSpec((1, LANES), const),
            pl.BlockSpec((1, LANES), const),
            pl.BlockSpec((1, HEAD_DIM), const),
            pl.BlockSpec((CHUNK, CHUNK), const),
        ],
        out_specs=pl.BlockSpec((tb, w), lambda b, i: (row(b, i), 0)),
        out_shape=jax.ShapeDtypeStruct((bsz * t_len, w), F32),
        scratch_shapes=[pltpu.VMEM((tb + SUBLANES, 3 * w), F32), pltpu.VMEM((tb, 3 * w), F32),
                        pltpu.VMEM((tb, LANES), F32), pltpu.VMEM((N_HEADS, HEAD_DIM, HEAD_DIM), F32)],
        compiler_params=_params(("parallel", "arbitrary")),
        name="gdn",
    )(y, y, y, conv_w, alog_row, dtb_row, norm_g, tri)


def _hgrn_masks():
    c = CHUNK
    levels = int(np.log2(c))
    m = np.zeros((levels + 2, c, c), np.float32)
    pm = np.zeros((levels + 1, c, c), np.float32)
    for l in range(levels):
        n = c >> (l + 1)
        for row in range(c):
            base = (row // (2 * n)) * 2 * n
            mid = base + n
            if row >= mid:
                m[l, row, mid:row + 1] = 1.0
            else:
                m[l, row, row + 1:mid] = 1.0
        for t in range(c):
            for s in range(c):
                if t // (2 * n) == s // (2 * n) and t % (2 * n) >= n and s % (2 * n) < n:
                    pm[l, t, s] = 1.0
    pm[levels] = np.eye(c, dtype=np.float32)
    m[levels] = np.tril(np.ones((c, c), np.float32))
    m[levels + 1] = np.triu(np.ones((c, c), np.float32), 1)
    return m.reshape((levels + 2) * c, c), pm


def _hgrn_kernel(x_ref, lb_ref, ng_ref, m_ref, pm_ref, o_ref, st_ref):
    tb = x_ref.shape[0]
    w = GROUP_WIDTH
    levels = pm_ref.shape[0] - 1

    @pl.when(pl.program_id(1) == 0)
    def _():
        st_ref[...] = jnp.zeros_like(st_ref)

    lb = lb_ref[...]
    ng = ng_ref[...]
    m_all = m_ref[...]

    def chunk(ci, carry):
        rows = pl.ds(pl.multiple_of(ci * CHUNK, CHUNK), CHUNK)
        fl = x_ref[rows, w:2 * w]
        e = jnp.exp(-jnp.abs(fl))
        rcp = 1.0 / (1.0 + e)
        sig = jnp.where(fl >= 0, rcp, e * rcp)
        nsig = jnp.where(fl >= 0, e * rcp, rcp)
        log_f = jnp.log(lb + (1.0 - lb) * sig)
        kk = (1.0 - lb) * nsig
        dsum = _dot(m_all, log_f, HI)
        for h in range(N_HEADS):
            cs = slice(h * HEAD_DIM, (h + 1) * HEAD_DIM)
            q = x_ref[rows, cs] * (HEAD_DIM ** -0.5)
            k = kk[:, cs]
            v = x_ref[rows, 2 * w + h * HEAD_DIM:2 * w + (h + 1) * HEAD_DIM]
            att = _dot_nt(q, k, HI) * pm_ref[levels]
            for l in range(levels):
                el = jnp.exp(dsum[l * CHUNK:(l + 1) * CHUNK, cs])
                att = att + _dot_nt(q * el, k * el, HI) * pm_ref[l]
            eb = jnp.exp(dsum[levels * CHUNK:(levels + 1) * CHUNK, cs])
            er = jnp.exp(dsum[(levels + 1) * CHUNK:(levels + 2) * CHUNK, cs])
            st = st_ref[h]
            o = _dot_nt(q * eb, st, HI) + _dot(att, v, HI)
            st_ref[h] = st * eb[CHUNK - 1:CHUNK, :] + _dot_tn(v, k * er, HI)
            gg = x_ref[rows, 3 * w + h * HEAD_DIM:3 * w + (h + 1) * HEAD_DIM]
            o_ref[rows, cs] = _rms(o, ng) * _silu(gg)
        return carry

    lax.fori_loop(0, tb // CHUNK, chunk, 0)


def _hgrn(y, bsz, t_len, lb_row, norm_g, m_all, pm):
    tb = min(MIX_BLOCK, t_len)
    nb = t_len // tb
    w = GROUP_WIDTH
    const2 = lambda b, i: (0, 0)
    return pl.pallas_call(
        _hgrn_kernel,
        grid=(bsz, nb),
        in_specs=[
            pl.BlockSpec((tb, 4 * w), lambda b, i: (b * nb + i, COL_HG // (4 * w))),
            pl.BlockSpec((1, w), const2),
            pl.BlockSpec((1, HEAD_DIM), const2),
            pl.BlockSpec(m_all.shape, const2),
            pl.BlockSpec(pm.shape, lambda b, i: (0, 0, 0)),
        ],
        out_specs=pl.BlockSpec((tb, w), lambda b, i: (b * nb + i, 0)),
        out_shape=jax.ShapeDtypeStruct((bsz * t_len, w), F32),
        scratch_shapes=[pltpu.VMEM((N_HEADS, HEAD_DIM, HEAD_DIM), F32)],
        compiler_params=_params(("parallel", "arbitrary")),
        name="hgrn2",
    )(y, lb_row, norm_g, m_all, pm)


def _s5_operators(lam_re, lam_im, log_dt, b_re, b_im, c_re, c_im, d_skip):
    g, n, p, el = S5_GROUPS, S5_STATE, S5_CH, S5_L
    gp = g // 2
    mm = functools.partial(jnp.einsum, precision=HI)
    dt = jnp.exp(log_dt)[:, None]
    lr, li = lam_re * dt, lam_im * dt
    mag = jnp.exp(lr)
    ab_re, ab_im = mag * jnp.cos(li), mag * jnp.sin(li)
    den = lam_re * lam_re + lam_im * lam_im
    nr = ab_re - 1.0
    cc_re = (nr * lam_re + ab_im * lam_im) / den
    cc_im = (ab_im * lam_re - nr * lam_im) / den
    bb_re = cc_re[..., None] * b_re - cc_im[..., None] * b_im
    bb_im = cc_re[..., None] * b_im + cc_im[..., None] * b_re
    ks = jnp.arange(el + 1, dtype=F32)[:, None, None]
    pw_re = jnp.exp(ks * lr) * jnp.cos(ks * li)
    pw_im = jnp.exp(ks * lr) * jnp.sin(ks * li)
    rp_re, rp_im = pw_re[el - 1::-1][:el], pw_im[el - 1::-1][:el]
    p_re = rp_re[..., None] * bb_re[None] - rp_im[..., None] * bb_im[None]
    p_im = rp_re[..., None] * bb_im[None] + rp_im[..., None] * bb_re[None]
    fp_re, fp_im = pw_re[1:], pw_im[1:]
    q_re = c_re[None] * fp_re[:, :, None, :] - c_im[None] * fp_im[:, :, None, :]
    q_im = c_re[None] * fp_im[:, :, None, :] + c_im[None] * fp_re[:, :, None, :]
    cp_re = c_re[None] * pw_re[:el, :, None, :] - c_im[None] * pw_im[:el, :, None, :]
    cp_im = c_re[None] * pw_im[:el, :, None, :] + c_im[None] * pw_re[:el, :, None, :]
    kern = mm('tgpn,gnq->tgpq', cp_re, bb_re) - mm('tgpn,gnq->tgpq', cp_im, bb_im)
    lag = jnp.arange(el)[None, :] - jnp.arange(el)[:, None]
    toep = jnp.where((lag >= 0)[:, :, None, None, None], kern[jnp.clip(lag, 0, el - 1)], 0.0)
    eye2 = jnp.eye(2, dtype=F32)
    p_re = jnp.einsum('lagnp,gh->algphn', p_re.reshape(el, gp, 2, n, p), eye2).reshape(gp, el * 2 * p, 2 * n)
    p_im = jnp.einsum('lagnp,gh->algphn', p_im.reshape(el, gp, 2, n, p), eye2).reshape(gp, el * 2 * p, 2 * n)
    q_re2 = jnp.einsum('lagpn,gh->agnlhp', q_re.reshape(el, gp, 2, p, n), eye2).reshape(gp, 2 * n, el * 2 * p)
    q_im2 = -jnp.einsum('lagpn,gh->agnlhp', q_im.reshape(el, gp, 2, p, n), eye2).reshape(gp, 2 * n, el * 2 * p)
    toep2 = jnp.einsum('slagpq,gh->asgqlhp', toep.reshape(el, el, gp, 2, p, p), eye2).reshape(gp, el * 2 * p, el * 2 * p)
    al_re = pw_re[el].reshape(gp, 1, 2 * n)
    al_im = pw_im[el].reshape(gp, 1, 2 * n)
    d2 = jnp.tile(d_skip.reshape(gp, 1, 2 * p), (1, el, 1)).reshape(gp, 1, el * 2 * p)
    return p_re, p_im, q_re2, q_im2, toep2, al_re, al_im, d2


def _s5_kernel(u_ref, pre_ref, pim_ref, qre_ref, qim_ref, toep_ref, alr_ref, ali_ref, d_ref, y_ref,
               zre, zim, *, bsz):
    u = u_ref[0]
    nrows = u.shape[0]
    zre[...] = _dot(u, pre_ref[0], HI)
    zim[...] = _dot(u, pim_ref[0], HI)
    ar = alr_ref[0]
    ai = ali_ref[0]
    per_tile = SUBLANES // bsz

    def tile(j, carry):
        xr, xi = carry
        rows = pl.ds(pl.multiple_of(j * SUBLANES, SUBLANES), SUBLANES)
        tr = zre[rows, :]
        ti = zim[rows, :]
        outr, outi = [], []
        for c in range(per_tile):
            outr.append(xr)
            outi.append(xi)
            nr = ar * xr - ai * xi + tr[c * bsz:(c + 1) * bsz, :]
            xi = ar * xi + ai * xr + ti[c * bsz:(c + 1) * bsz, :]
            xr = nr
        zre[rows, :] = jnp.concatenate(outr, axis=0)
        zim[rows, :] = jnp.concatenate(outi, axis=0)
        return xr, xi

    zero = jnp.zeros((bsz, zre.shape[1]), F32)
    lax.fori_loop(0, nrows // SUBLANES, tile, (zero, zero))
    y = (_dot(u, toep_ref[0], HI) + _dot(zre[...], qre_ref[0], HI) + _dot(zim[...], qim_ref[0], HI)
         + d_ref[0] * u)
    y_ref[0] = _gelu(y)


def _s5(u_rows, ops, bsz):
    gp, nrows, feat = u_rows.shape
    p_re, p_im, q_re, q_im, toep, al_re, al_im, d2 = ops
    ns = p_re.shape[2]
    blk = lambda a: pl.BlockSpec((1,) + a.shape[1:], lambda i: (i, 0, 0))
    return pl.pallas_call(
        functools.partial(_s5_kernel, bsz=bsz),
        grid=(gp,),
        in_specs=[blk(u_rows), blk(p_re), blk(p_im), blk(q_re), blk(q_im), blk(toep), blk(al_re), blk(al_im),
                  blk(d2)],
        out_specs=pl.BlockSpec((1, nrows, feat), lambda i: (i, 0, 0)),
        out_shape=jax.ShapeDtypeStruct((gp, nrows, feat), F32),
        scratch_shapes=[pltpu.VMEM((nrows, ns), F32), pltpu.VMEM((nrows, ns), F32)],
        compiler_params=_params(("parallel",)),
        name="s5",
    )(u_rows, p_re, p_im, q_re, q_im, toep, al_re, al_im, d2)


def _lru_kernel(x_ref, gate_ref, cw_ref, cb_ref, wa_ref, ba_ref, wx_ref, bx_ref, lp_ref, ng_ref, o_ref,
                xe_ref, a_s, v_s, h_ref):
    tb = x_ref.shape[0]
    first = pl.program_id(1) == 0

    @pl.when(first)
    def _():
        h_ref[...] = jnp.zeros_like(h_ref)

    xc = _causal_conv(x_ref[...], cw_ref, xe_ref, first) + cb_ref[...]
    r = _sigmoid(_dot(xc, wa_ref[...]) + ba_ref[...])
    gi = _sigmoid(_dot(xc, wx_ref[...]) + bx_ref[...])
    log_a = -LRU_C * r * _softplus(-lp_ref[...])
    a = jnp.exp(log_a)
    th = jnp.tanh(log_a)
    v = jnp.sqrt(-2.0 * th / (1.0 - th)) * gi * xc
    rowmod = lax.broadcasted_iota(jnp.int32, a.shape, 0) % SUBLANES
    for dist in (1, 2, 4):
        keep = rowmod >= dist
        a_sh = jnp.where(keep, pltpu.roll(a, dist, 0), 1.0)
        v_sh = jnp.where(keep, pltpu.roll(v, dist, 0), 0.0)
        v = v + a * v_sh
        a = a * a_sh
    a_s[...] = a
    v_s[...] = v
    h = h_ref[...]
    for t in range(tb // SUBLANES):
        rows = slice(t * SUBLANES, (t + 1) * SUBLANES)
        ht = v_s[rows, :] + a_s[rows, :] * h
        v_s[rows, :] = ht
        h = ht[SUBLANES - 1:SUBLANES, :]
    h_ref[...] = h
    o_ref[...] = _rms(v_s[...] * _gelu(gate_ref[...]), ng_ref[...])


def _lru(y, bsz, t_len, conv_w, conv_b, wa_bd, ba, wx_bd, bx, lparam, norm_g):
    tb = min(MIX_BLOCK, t_len)
    nb = t_len // tb
    w = GROUP_WIDTH
    const = lambda b, i: (0, 0)
    vec = pl.BlockSpec((1, w), const)
    return pl.pallas_call(
        _lru_kernel,
        grid=(bsz, nb),
        in_specs=[
            pl.BlockSpec((tb, w), lambda b, i: (b * nb + i, COL_LX // w)),
            pl.BlockSpec((tb, w), lambda b, i: (b * nb + i, COL_LG // w)),
            pl.BlockSpec((CONV_WIDTH, w), const), vec,
            pl.BlockSpec((w, w), const), vec,
            pl.BlockSpec((w, w), const), vec, vec, vec,
        ],
        out_specs=pl.BlockSpec((tb, w), lambda b, i: (b * nb + i, 0)),
        out_shape=jax.ShapeDtypeStruct((bsz * t_len, w), F32),
        scratch_shapes=[pltpu.VMEM((tb + SUBLANES, w), F32), pltpu.VMEM((tb, w), F32),
                        pltpu.VMEM((tb, w), F32), pltpu.VMEM((1, w), F32)],
        compiler_params=_params(("parallel", "arbitrary")),
        name="rglru",
    )(y, y, conv_w, conv_b, wa_bd, ba, wx_bd, bx, lparam, norm_g)


def _mix_out_kernel(x_ref, ya_ref, yb_ref, yc_ref, yd_ref, gw_ref, gb_ref, cg_ref, wo_ref, g_ref, b_ref, o_ref):
    w = GROUP_WIDTH
    yc = yc_ref[...]
    yc = yc * _sigmoid(_dot(yc, gw_ref[...]) + gb_ref[...])
    yc = _rms(yc, cg_ref[...])
    acc = _dot(ya_ref[...].astype(BF16), wo_ref[0:w, :])
    acc += _dot(yb_ref[...].astype(BF16), wo_ref[w:2 * w, :])
    acc += _dot(yc.astype(BF16), wo_ref[2 * w:3 * w, :])
    acc += _dot(yd_ref[...].astype(BF16), wo_ref[3 * w:4 * w, :])
    o_ref[...] = _layer_norm(DN_ALPHA * x_ref[...] + acc, g_ref[...], b_ref[...])


def _mix_out(x, ya, yb, yc, yd, glu_w, glu_b, c_norm_g, w_out, g, b, tm=256):
    n, d = x.shape
    w = GROUP_WIDTH
    tm = min(tm, n)
    const = lambda i: (0, 0)
    tok = lambda width: pl.BlockSpec((tm, width), lambda i: (i, 0))
    return pl.pallas_call(
        _mix_out_kernel,
        grid=(n // tm,),
        in_specs=[tok(d), tok(w), tok(w), tok(w), tok(w),
                  pl.BlockSpec((w, w), const), pl.BlockSpec((1, w), const), pl.BlockSpec((1, w), const),
                  pl.BlockSpec((d, d), const), pl.BlockSpec((1, d), const), pl.BlockSpec((1, d), const)],
        out_specs=tok(d),
        out_shape=jax.ShapeDtypeStruct((n, d), F32),
        compiler_params=_params(("parallel",)),
        name="mix_out",
    )(x, ya, yb, yc, yd, glu_w, glu_b, c_norm_g, w_out, g, b)


def _ple_kernel(x_ref, p_ref, wp_ref, wg_ref, g_ref, b_ref, o_ref):
    x = x_ref[...]
    gate = _sigmoid(_dot(x.astype(BF16), wg_ref[...]))
    ple = _dot(p_ref[...].astype(BF16), wp_ref[...]) * gate
    o_ref[...] = _layer_norm(DN_ALPHA * x + ple, g_ref[...], b_ref[...])


def _ple(x, p, wp, wg, g, b, tm=256):
    n, d = x.shape
    tm = min(tm, n)
    const = lambda i: (0, 0)
    return pl.pallas_call(
        _ple_kernel,
        grid=(n // tm,),
        in_specs=[pl.BlockSpec((tm, d), lambda i: (i, 0)), pl.BlockSpec((tm, p.shape[1]), lambda i: (i, 0)),
                  pl.BlockSpec(wp.shape, const), pl.BlockSpec(wg.shape, const),
                  pl.BlockSpec((1, d), const), pl.BlockSpec((1, d), const)],
        out_specs=pl.BlockSpec((tm, d), lambda i: (i, 0)),
        out_shape=jax.ShapeDtypeStruct((n, d), F32),
        compiler_params=_params(("parallel",)),
        name="ple",
    )(x, p, wp, wg, g, b)


def _block_diag(wb):
    nb, d, _ = wb.shape
    return jnp.einsum('nde,nm->ndme', wb, jnp.eye(nb, dtype=wb.dtype)).reshape(nb * d, nb * d)


def _pad_row(vals, offset):
    return jnp.zeros((1, LANES), F32).at[0, offset:offset + vals.shape[0]].set(vals)


def _token_mix(h, bsz, t_len, w_in_p, w_out, gdn_conv_w, gdn_A_log, gdn_dt_bias, gdn_norm_g, lower_bound,
               hgrn_norm_g, s5_ops, s5_glu_w, s5_glu_b, lru_conv_w, lru_conv_b, lru_wa, lru_ba, lru_wx, lru_bx,
               lru_param, branch_norm_g, ln_g, ln_b, consts):
    w = GROUP_WIDTH
    n = bsz * t_len
    tri, m_all, pm = consts
    y = _proj(h, w_in_p)
    ya = _gdn(y, bsz, t_len, gdn_conv_w, _pad_row(gdn_A_log, N_HEADS), _pad_row(gdn_dt_bias, N_HEADS),
              gdn_norm_g.reshape(1, HEAD_DIM), tri)
    yb = _hgrn(y, bsz, t_len, lower_bound.reshape(1, w), hgrn_norm_g.reshape(1, HEAD_DIM), m_all, pm)
    nc = t_len // S5_L
    gp = S5_GROUPS // 2
    u = y[:, COL_S5:COL_S5 + w].reshape(bsz, nc, S5_L, gp, 2 * S5_CH)
    u = u.transpose(3, 1, 0, 2, 4).reshape(gp, nc * bsz, S5_L * 2 * S5_CH)
    yc = _s5(u, s5_ops, bsz)
    yc = yc.reshape(gp, nc, bsz, S5_L, 2 * S5_CH).transpose(2, 1, 3, 0, 4).reshape(n, w)
    yd = _lru(y, bsz, t_len, lru_conv_w, lru_conv_b.reshape(1, w), _block_diag(lru_wa), lru_ba.reshape(1, w),
              _block_diag(lru_wx), lru_bx.reshape(1, w), lru_param.reshape(1, w), branch_norm_g[1].reshape(1, w))
    return _mix_out(h, ya, yb, yc, yd, s5_glu_w, s5_glu_b.reshape(1, w), branch_norm_g[0].reshape(1, w),
                    w_out, ln_g, ln_b)


def kernel(x, p, ln_g, ln_b, ffn_wi, ffn_wo, mix_w_in, mix_w_out, gdn_conv_w, gdn_A_log, gdn_dt_bias, gdn_norm_g, hgrn_lb_logits, hgrn_norm_g, s5_lam_re, s5_lam_im, s5_log_dt, s5_B_re, s5_B_im, s5_C_re, s5_C_im, s5_D, s5_glu_w, s5_glu_b, lru_conv_w, lru_conv_b, lru_wa, lru_ba, lru_wx, lru_bx, lru_param, branch_norm_g, ple_w, ple_gate_w):
    bsz, t_len, d = x.shape
    n = bsz * t_len
    depth = ln_g.shape[0]
    lb_cum = jnp.cumsum(jax.nn.softmax(hgrn_lb_logits.astype(F32), axis=0), axis=0)
    lower_bounds = lb_cum - lb_cum[0:1]
    m_all, pm = _hgrn_masks()
    consts = (jnp.tril(jnp.ones((CHUNK, CHUNK), F32)), jnp.asarray(m_all), jnp.asarray(pm))
    h = x.reshape(n, d)
    vec = lambda a: a.reshape(1, d)
    for i in range(depth):
        w_in = mix_w_in[i]
        w_in_p = jnp.concatenate(
            [w_in[:, :4 * GROUP_WIDTH], w_in[:, 4 * GROUP_WIDTH + 2 * N_HEADS:],
             w_in[:, 4 * GROUP_WIDTH:4 * GROUP_WIDTH + 2 * N_HEADS],
             jnp.zeros((d, IN_COLS_PAD - w_in.shape[1]), w_in.dtype)], axis=1).astype(BF16)
        s5_ops = _s5_operators(s5_lam_re[i], s5_lam_im[i], s5_log_dt[i], s5_B_re[i], s5_B_im[i], s5_C_re[i],
                               s5_C_im[i], s5_D[i])
        h = _ffn(h, ffn_wi[i, 0].astype(BF16), ffn_wo[i, 0].astype(BF16), vec(ln_g[i, 0]), vec(ln_b[i, 0]))
        h = _token_mix(h, bsz, t_len, w_in_p, mix_w_out[i].astype(BF16), gdn_conv_w[i], gdn_A_log[i],
                       gdn_dt_bias[i], gdn_norm_g[i], lower_bounds[i], hgrn_norm_g[i], s5_ops, s5_glu_w[i],
                       s5_glu_b[i], lru_conv_w[i], lru_conv_b[i], lru_wa[i], lru_ba[i], lru_wx[i], lru_bx[i],
                       lru_param[i], branch_norm_g[i], vec(ln_g[i, 1]), vec(ln_b[i, 1]), consts)
        h = _ffn(h, ffn_wi[i, 1].astype(BF16), ffn_wo[i, 1].astype(BF16), vec(ln_g[i, 2]), vec(ln_b[i, 2]))
        h = _ple(h, p[i].reshape(n, p.shape[-1]), ple_w[i].astype(BF16), ple_gate_w[i].astype(BF16),
                 vec(ln_g[i, 3]), vec(ln_b[i, 3]))
    return h.reshape(bsz, t_len, d)
```

```python
import functools

import numpy as np
import jax
import jax.numpy as jnp
from jax import lax
from jax.experimental import pallas as pl
from jax.experimental.pallas import tpu as pltpu

F32 = jnp.float32
BF16 = jnp.bfloat16
HI = lax.Precision.HIGHEST

D_MODEL = 2048
DEPTH = 2
GROUP_WIDTH = D_MODEL // 4
HEAD_DIM = 128
N_HEADS = GROUP_WIDTH // HEAD_DIM
S5_CH = 16
S5_GROUPS = GROUP_WIDTH // S5_CH
S5_STATE = 64
LRU_BLOCKS = 8
LRU_C = 8.0
CONV_WIDTH = 4
D_FF = ((8 * D_MODEL // 3 + 255) // 256) * 256
PLE_DIM = 256
DN_ALPHA = (2 * DEPTH) ** 0.25
LN_EPS = 1e-5
RMS_EPS = 1e-6

SUBLANES = 8
LANES = 128
VMEM_LIMIT = 56 * 1024 * 1024

CHUNK = 64
MIX_BLOCK = 256
S5_L = 16
IN_COLS_PAD = 6144
COL_QKV, COL_Z, COL_HG, COL_S5, COL_LX, COL_LG, COL_BD = 0, 1536, 2048, 4096, 4608, 5120, 5632


def _params(sem):
    return pltpu.CompilerParams(dimension_semantics=sem, vmem_limit_bytes=VMEM_LIMIT)


def _sigmoid(x):
    return 1.0 / (1.0 + jnp.exp(-x))


def _silu(x):
    return x * _sigmoid(x)


def _softplus(x):
    return jnp.maximum(x, 0.0) + jnp.log(1.0 + jnp.exp(-jnp.abs(x)))


def _gelu(x):
    return 0.5 * x * (1.0 + jnp.tanh(0.7978845608028654 * (x + 0.044715 * x * x * x)))


def _layer_norm(y, g, b):
    mu = jnp.mean(y, axis=-1, keepdims=True)
    yc = y - mu
    var = jnp.mean(yc * yc, axis=-1, keepdims=True)
    return yc * lax.rsqrt(var + LN_EPS) * g + b


def _rms(y, g):
    return y * lax.rsqrt(jnp.mean(y * y, axis=-1, keepdims=True) + RMS_EPS) * g


def _dot(a, b, precision=None):
    return jnp.dot(a, b, preferred_element_type=F32, precision=precision)


def _dot_nt(a, b, precision=None):
    return lax.dot_general(a, b, (((1,), (1,)), ((), ())), preferred_element_type=F32, precision=precision)


def _dot_tn(a, b, precision=None):
    return lax.dot_general(a, b, (((0,), (0,)), ((), ())), preferred_element_type=F32, precision=precision)


def _bdot(a, b, ca, cb):
    return lax.dot_general(a, b, (((ca,), (cb,)), ((0,), (0,))), preferred_element_type=F32)


def _bf(x):
    return x.astype(BF16)


def _dot_split(m, x):
    x1 = x.astype(BF16)
    r1 = x - x1.astype(F32)
    x2 = r1.astype(BF16)
    x3 = (r1 - x2.astype(F32)).astype(BF16)
    return _dot(m, x3) + _dot(m, x2) + _dot(m, x1)


def _ffn_kernel(x_ref, wg_ref, wu_ref, wo_ref, g_ref, b_ref, o_ref, xb_ref):
    f = pl.program_id(1)

    @pl.when(f == 0)
    def _():
        xb_ref[...] = x_ref[...].astype(BF16)
        o_ref[...] = jnp.zeros_like(o_ref)

    xb = xb_ref[...]
    hg = _dot(xb, wg_ref[...])
    hu = _dot(xb, wu_ref[...])
    o_ref[...] += _dot((_silu(hg) * hu).astype(BF16), wo_ref[...])

    @pl.when(f == pl.num_programs(1) - 1)
    def _():
        y = DN_ALPHA * x_ref[...] + 0.5 * o_ref[...]
        o_ref[...] = _layer_norm(y, g_ref[...], b_ref[...])


def _ffn(x, wi, wo, g, b, tm=1024, tf=512):
    n, d = x.shape
    dff = wo.shape[0]
    tm = min(tm, n)
    nf = dff // tf
    return pl.pallas_call(
        _ffn_kernel,
        grid=(n // tm, nf),
        in_specs=[
            pl.BlockSpec((tm, d), lambda i, f: (i, 0), pipeline_mode=pl.Buffered(1)),
            pl.BlockSpec((d, tf), lambda i, f: (0, f)),
            pl.BlockSpec((d, tf), lambda i, f: (0, f + nf)),
            pl.BlockSpec((tf, d), lambda i, f: (f, 0)),
            pl.BlockSpec((1, d), lambda i, f: (0, 0)),
            pl.BlockSpec((1, d), lambda i, f: (0, 0)),
        ],
        out_specs=pl.BlockSpec((tm, d), lambda i, f: (i, 0)),
        out_shape=jax.ShapeDtypeStruct((n, d), F32),
        scratch_shapes=[pltpu.VMEM((tm, d), BF16)],
        compiler_params=_params(("parallel", "arbitrary")),
        name="ffn",
    )(x, wi, wi, wo, g, b)


def _proj_kernel(x_ref, w_ref, o_ref, xb_ref):
    @pl.when(pl.program_id(1) == 0)
    def _():
        xb_ref[...] = x_ref[...].astype(BF16)

    o_ref[...] = _dot(xb_ref[...], w_ref[...])


def _proj(x, w, tm=1024, tn=1024):
    n, d = x.shape
    cols = w.shape[1]
    tm = min(tm, n)
    return pl.pallas_call(
        _proj_kernel,
        grid=(n // tm, cols // tn),
        in_specs=[pl.BlockSpec((tm, d), lambda i, j: (i, 0)),
                  pl.BlockSpec((d, tn), lambda i, j: (0, j))],
        out_specs=pl.BlockSpec((tm, tn), lambda i, j: (i, j)),
        out_shape=jax.ShapeDtypeStruct((n, cols), F32),
        scratch_shapes=[pltpu.VMEM((tm, d), BF16)],
        compiler_params=_params(("parallel", "arbitrary")),
        name="in_proj",
    )(x, w)


def _causal_conv(x, w_ref, xe_ref, first):
    tb = x.shape[0]

    @pl.when(first)
    def _():
        xe_ref[0:SUBLANES, :] = jnp.zeros((SUBLANES, x.shape[1]), F32)

    xe_ref[SUBLANES:SUBLANES + tb, :] = x
    acc = x * w_ref[CONV_WIDTH - 1:CONV_WIDTH, :]
    for j in range(CONV_WIDTH - 1):
        off = SUBLANES - (CONV_WIDTH - 1) + j
        acc = acc + xe_ref[off:off + tb, :] * w_ref[j:j + 1, :]
    xe_ref[0:SUBLANES, :] = xe_ref[tb:tb + SUBLANES, :]
    return acc


def _gdn_kernel(qkv_ref, z_ref, bd_ref, cw_ref, alog_ref, dtb_ref, ng_ref, tri_ref, o_ref, xe_ref, s_ref):
    tb = qkv_ref.shape[0]
    w = GROUP_WIDTH
    nch = tb // CHUNK
    first = pl.program_id(1) == 0

    @pl.when(first)
    def _():
        s_ref[...] = jnp.zeros_like(s_ref)

    qkv = _silu(_causal_conv(qkv_ref[...], cw_ref, xe_ref, first))
    bd = bd_ref[...]
    beta_all = _sigmoid(bd)
    g_all = -jnp.exp(alog_ref[...]) * _softplus(bd + dtb_ref[...])
    tri = tri_ref[...]

    def stack(col0):
        parts = []
        for ci in range(nch):
            for h in range(N_HEADS):
                parts.append(qkv[ci * CHUNK:(ci + 1) * CHUNK, col0 + h * HEAD_DIM:col0 + (h + 1) * HEAD_DIM][None])
        return jnp.concatenate(parts, axis=0)

    q, k, v = stack(0), stack(w), stack(2 * w)
    q = q * (lax.rsqrt(jnp.sum(q * q, axis=-1, keepdims=True) + RMS_EPS) * (HEAD_DIM ** -0.5))
    k = k * lax.rsqrt(jnp.sum(k * k, axis=-1, keepdims=True) + RMS_EPS)
    gcs, grs, bcs = [], [], []
    for ci in range(nch):
        rows = slice(ci * CHUNK, (ci + 1) * CHUNK)
        gcum = _dot_split(tri, g_all[rows, :])
        gcum_t = gcum.T
        for h in range(N_HEADS):
            gcs.append(gcum[:, N_HEADS + h:N_HEADS + h + 1][None])
            grs.append(gcum_t[N_HEADS + h:N_HEADS + h + 1, :][None])
            bcs.append(beta_all[rows, h:h + 1][None])
    gc = jnp.concatenate(gcs, axis=0)
    gr = jnp.concatenate(grs, axis=0)
    bc = jnp.concatenate(bcs, axis=0)

    r = lax.broadcasted_iota(jnp.int32, (1, CHUNK, CHUNK), 1)
    c = lax.broadcasted_iota(jnp.int32, (1, CHUNK, CHUNK), 2)
    eye = (r == c).astype(F32)
    dec = jnp.exp(jnp.where(r >= c, gc - gr, -jnp.inf))
    kb = _bf(k)
    a = bc * _bdot(kb, kb, 2, 2) * jnp.where(r > c, dec, 0.0)
    qk = _bdot(_bf(q), kb, 2, 2) * dec
    t = eye - a
    p = _bdot(_bf(a), _bf(a), 2, 1)
    n = 2
    while True:
        t = t + _bdot(_bf(t), _bf(p), 2, 1)
        n *= 2
        if n >= CHUNK:
            break
        p = _bdot(_bf(p), _bf(p), 2, 1)
    eg = jnp.exp(gc)
    rhs = jnp.concatenate([bc * v, (bc * eg) * k], axis=2)
    solb = _bf(rhs + _bdot(_bf(t - eye), _bf(rhs), 2, 1))
    glast = gc[:, CHUNK - 1:CHUNK, :]
    kd = k * jnp.exp(glast - gc)
    nk = _bdot(_bf(kd), solb, 1, 1)
    qw = _bdot(_bf(qk), solb, 2, 1)
    qeff = _bf(q * eg - qw[:, :, HEAD_DIM:])
    nku = nk[:, :, :HEAD_DIM]
    nkw = _bf(nk[:, :, HEAD_DIM:])
    gl = jnp.exp(glast)
    ng = ng_ref[...]

    s = s_ref[...]
    for ci in range(nch):
        sel = slice(ci * N_HEADS, (ci + 1) * N_HEADS)
        sb = _bf(s)
        o = qw[sel, :, :HEAD_DIM] + _bdot(qeff[sel], sb, 2, 1)
        s = gl[sel] * s + nku[sel] - _bdot(nkw[sel], sb, 2, 1)
        rows = slice(ci * CHUNK, (ci + 1) * CHUNK)
        for h in range(N_HEADS):
            hs = slice(h * HEAD_DIM, (h + 1) * HEAD_DIM)
            o_ref[rows, hs] = _rms(o[h], ng) * _silu(z_ref[rows, hs])
    s_ref[...] = s


def _gdn(y, bsz, t_len, conv_w, alog_row, dtb_row, norm_g, tri):
    tb = min(MIX_BLOCK, t_len)
    nb = t_len // tb
    w = GROUP_WIDTH
    row = lambda b, i: b * nb + i
    const = lambda b, i: (0, 0)
    return pl.pallas_call(
        _gdn_kernel,
        grid=(bsz, nb),
        in_specs=[
            pl.BlockSpec((tb, 3 * w), lambda b, i: (row(b, i), COL_QKV // (3 * w))),
            pl.BlockSpec((tb, w), lambda b, i: (row(b, i), COL_Z // w)),
            pl.BlockSpec((tb, LANES), lambda b, i: (row(b, i), COL_BD // LANES)),
            pl.BlockSpec((CONV_WIDTH, 3 * w), const),
            pl.BlockSpec((1, LANES), const),
            pl.BlockSpec((1, LANES), const),
            pl.BlockSpec((1, HEAD_DIM), const),
            pl.BlockSpec((CHUNK, CHUNK), const),
        ],
        out_specs=pl.BlockSpec((tb, w), lambda b, i: (row(b, i), 0)),
        out_shape=jax.ShapeDtypeStruct((bsz * t_len, w), F32),
        scratch_shapes=[pltpu.VMEM((tb + SUBLANES, 3 * w), F32),
                        pltpu.VMEM((N_HEADS, HEAD_DIM, HEAD_DIM), F32)],
        compiler_params=_params(("parallel", "arbitrary")),
        name="gdn",
    )(y, y, y, conv_w, alog_row, dtb_row, norm_g, tri)


def _hgrn_masks():
    c = CHUNK
    levels = int(np.log2(c))
    m = np.zeros((levels + 2, c, c), np.float32)
    pm = np.zeros((levels + 1, c, c), np.float32)
    for l in range(levels):
        n = c >> (l + 1)
        for row in range(c):
            base = (row // (2 * n)) * 2 * n
            mid = base + n
            if row >= mid:
                m[l, row, mid:row + 1] = 1.0
            else:
                m[l, row, row + 1:mid] = 1.0
        for t in range(c):
            for s in range(c):
                if t // (2 * n) == s // (2 * n) and t % (2 * n) >= n and s % (2 * n) < n:
                    pm[l, t, s] = 1.0
    pm[levels] = np.eye(c, dtype=np.float32)
    m[levels] = np.tril(np.ones((c, c), np.float32))
    m[levels + 1] = np.triu(np.ones((c, c), np.float32), 1)
    return m.reshape((levels + 2) * c, c), pm


def _hgrn_kernel(x_ref, lb_ref, ng_ref, m_ref, pm_ref, o_ref, st_ref):
    tb = x_ref.shape[0]
    w = GROUP_WIDTH
    nch = tb // CHUNK
    levels = pm_ref.shape[0] - 1

    @pl.when(pl.program_id(1) == 0)
    def _():
        st_ref[...] = jnp.zeros_like(st_ref)

    lb = lb_ref[...]
    ng = ng_ref[...]
    m_all = m_ref[...]
    fl = x_ref[:, w:2 * w]
    e = jnp.exp(-jnp.abs(fl))
    rcp = 1.0 / (1.0 + e)
    sig = jnp.where(fl >= 0, rcp, e * rcp)
    nsig = jnp.where(fl >= 0, e * rcp, rcp)
    log_f = jnp.log(lb + (1.0 - lb) * sig)
    kk = (1.0 - lb) * nsig
    e_all = [jnp.exp(_dot_split(m_all, log_f[ci * CHUNK:(ci + 1) * CHUNK, :])) for ci in range(nch)]

    def stack(get):
        return jnp.concatenate([get(ci, h)[None] for ci in range(nch) for h in range(N_HEADS)], axis=0)

    blk = lambda a, ci, h: a[ci * CHUNK:(ci + 1) * CHUNK, h * HEAD_DIM:(h + 1) * HEAD_DIM]
    lvl = lambda l: stack(lambda ci, h: e_all[ci][l * CHUNK:(l + 1) * CHUNK, h * HEAD_DIM:(h + 1) * HEAD_DIM])
    q = stack(lambda ci, h: x_ref[ci * CHUNK:(ci + 1) * CHUNK, h * HEAD_DIM:(h + 1) * HEAD_DIM]) * (HEAD_DIM ** -0.5)
    k = stack(lambda ci, h: blk(kk, ci, h))
    vb = _bf(stack(lambda ci, h: x_ref[ci * CHUNK:(ci + 1) * CHUNK, 2 * w + h * HEAD_DIM:2 * w + (h + 1) * HEAD_DIM]))
    att = _bdot(_bf(q), _bf(k), 2, 2) * pm_ref[levels][None]
    for l in range(levels):
        el = lvl(l)
        att = att + _bdot(_bf(q * el), _bf(k * el), 2, 2) * pm_ref[l][None]
    eb = lvl(levels)
    er = lvl(levels + 1)
    o_intra = _bdot(_bf(att), vb, 2, 1)
    upd = _bdot(vb, _bf(k * er), 1, 1)
    qe = _bf(q * eb)
    dlast = eb[:, CHUNK - 1:CHUNK, :]

    st = st_ref[...]
    for ci in range(nch):
        sel = slice(ci * N_HEADS, (ci + 1) * N_HEADS)
        o = o_intra[sel] + _bdot(qe[sel], _bf(st), 2, 2)
        st = st * dlast[sel] + upd[sel]
        rows = slice(ci * CHUNK, (ci + 1) * CHUNK)
        for h in range(N_HEADS):
            cs = slice(h * HEAD_DIM, (h + 1) * HEAD_DIM)
            gg = x_ref[rows, 3 * w + h * HEAD_DIM:3 * w + (h + 1) * HEAD_DIM]
            o_ref[rows, cs] = _rms(o[h], ng) * _silu(gg)
    st_ref[...] = st


def _hgrn(y, bsz, t_len, lb_row, norm_g, m_all, pm):
    tb = min(MIX_BLOCK, t_len)
    nb = t_len // tb
    w = GROUP_WIDTH
    const2 = lambda b, i: (0, 0)
    return pl.pallas_call(
        _hgrn_kernel,
        grid=(bsz, nb),
        in_specs=[
            pl.BlockSpec((tb, 4 * w), lambda b, i: (b * nb + i, COL_HG // (4 * w))),
            pl.BlockSpec((1, w), const2),
            pl.BlockSpec((1, HEAD_DIM), const2),
            pl.BlockSpec(m_all.shape, const2),
            pl.BlockSpec(pm.shape, lambda b, i: (0, 0, 0)),
        ],
        out_specs=pl.BlockSpec((tb, w), lambda b, i: (b * nb + i, 0)),
        out_shape=jax.ShapeDtypeStruct((bsz * t_len, w), F32),
        scratch_shapes=[pltpu.VMEM((N_HEADS, HEAD_DIM, HEAD_DIM), F32)],
        compiler_params=_params(("parallel", "arbitrary")),
        name="hgrn2",
    )(y, lb_row, norm_g, m_all, pm)


S5_PAIR = 2 * S5_CH
S5_PAIRS = S5_GROUPS // 2
S5_FEAT = S5_L * S5_PAIR
S5_NS = 2 * S5_STATE
S5_QUADS = LANES // S5_PAIR


def _hi_lo(a):
    hi = a.astype(BF16)
    return hi, (a - hi.astype(F32)).astype(BF16)


def _s5_operators(lam_re, lam_im, log_dt, b_re, b_im, c_re, c_im, d_skip):
    dep = lam_re.shape[0]
    n, p, el, gp = S5_STATE, S5_CH, S5_L, S5_PAIRS
    mm = functools.partial(jnp.einsum, precision=HI)
    dt = jnp.exp(log_dt)[..., None]
    lr, li = lam_re * dt, lam_im * dt
    mag = jnp.exp(lr)
    ab_re, ab_im = mag * jnp.cos(li), mag * jnp.sin(li)
    den = lam_re * lam_re + lam_im * lam_im
    nr = ab_re - 1.0
    cc_re = (nr * lam_re + ab_im * lam_im) / den
    cc_im = (ab_im * lam_re - nr * lam_im) / den
    bb_re = cc_re[..., None] * b_re - cc_im[..., None] * b_im
    bb_im = cc_re[..., None] * b_im + cc_im[..., None] * b_re
    ks = jnp.arange(el + 1, dtype=F32)[:, None, None, None]
    pw_re = jnp.exp(ks * lr[None]) * jnp.cos(ks * li[None])
    pw_im = jnp.exp(ks * lr[None]) * jnp.sin(ks * li[None])
    rp_re, rp_im = pw_re[el - 1::-1], pw_im[el - 1::-1]
    p_re = rp_re[..., None] * bb_re[None] - rp_im[..., None] * bb_im[None]
    p_im = rp_re[..., None] * bb_im[None] + rp_im[..., None] * bb_re[None]
    fp_re, fp_im = pw_re[1:el + 1, :, :, None, :], pw_im[1:el + 1, :, :, None, :]
    q_re = c_re[None] * fp_re - c_im[None] * fp_im
    q_im = c_re[None] * fp_im + c_im[None] * fp_re
    cp_re = c_re[None] * pw_re[:el, :, :, None, :] - c_im[None] * pw_im[:el, :, :, None, :]
    cp_im = c_re[None] * pw_im[:el, :, :, None, :] + c_im[None] * pw_re[:el, :, :, None, :]
    kern = mm('tdgpn,dgnq->tdgpq', cp_re, bb_re) - mm('tdgpn,dgnq->tdgpq', cp_im, bb_im)
    lag = jnp.arange(el)[None, :] - jnp.arange(el)[:, None]
    toep = jnp.where((lag >= 0)[:, :, None, None, None, None], kern[jnp.clip(lag, 0, el - 1)], 0.0)
    eye2 = jnp.eye(2, dtype=F32)
    pz = jnp.concatenate(
        [jnp.einsum('ldagnp,gh->dalgphn', t.reshape(el, dep, gp, 2, n, p), eye2).reshape(dep, gp, S5_FEAT, S5_NS)
         for t in (p_re, p_im)], axis=-1)
    qe = jnp.concatenate(
        [jnp.einsum('ldagpn,gh->dagnlhp', t.reshape(el, dep, gp, 2, p, n), eye2).reshape(dep, gp, S5_NS, S5_FEAT)
         for t in (q_re, -q_im)], axis=-2)
    tp = jnp.einsum('sldagpq,gh->dasgqlhp', toep.reshape(el, el, dep, gp, 2, p, p), eye2)
    tp = tp.reshape(dep, gp, S5_FEAT, S5_FEAT)
    cmul = lambda x, y: (x[0] * y[0] - x[1] * y[1], x[0] * y[1] + x[1] * y[0])
    a1 = (pw_re[el], pw_im[el])
    a2 = cmul(a1, a1)
    a4 = cmul(a2, a2)
    a3 = cmul(a2, a1)
    pows = [(jnp.ones_like(a1[0]), jnp.zeros_like(a1[0])), a1, a2, a3, a4, cmul(a4, a1), cmul(a4, a2), cmul(a4, a3)]
    a8 = cmul(a4, a4)
    rows = [t[0] for t in pows] + [t[1] for t in pows] + [a1[0], a1[1], a2[0], a2[1], a4[0], a4[1], a8[0], a8[1]]
    coef = jnp.stack([t.reshape(dep, gp, S5_NS) for t in rows], axis=2)
    d2 = jnp.tile(d_skip.reshape(dep, gp, 1, S5_PAIR), (1, 1, el, 1)).reshape(dep, gp, 1, S5_FEAT)
    return _hi_lo(pz) + _hi_lo(qe) + _hi_lo(tp) + (coef, d2)


def _dot3(x, w_hi, w_lo):
    x_hi = x.astype(BF16)
    x_lo = (x - x_hi.astype(F32)).astype(BF16)
    return _dot(x_lo, w_hi) + _dot(x_hi, w_lo) + _dot(x_hi, w_hi)


def _s5_kernel(u_ref, pzh_ref, pzl_ref, qeh_ref, qel_ref, tph_ref, tpl_ref, cf_ref, d_ref, y_ref):
    nc = u_ref.shape[0]
    ns = S5_NS
    tok = [u_ref[:, l, :] for l in range(S5_L)]
    rowmod = lax.broadcasted_iota(jnp.int32, (nc, ns), 0) % SUBLANES
    ys = []
    for qd in range(S5_QUADS):
        lanes = slice(qd * S5_PAIR, (qd + 1) * S5_PAIR)
        u = jnp.concatenate([t[:, lanes] for t in tok], axis=1)
        cf = cf_ref[0, qd]
        z = _dot3(u, pzh_ref[0, qd], pzl_ref[0, qd])
        zr, zi = z[:, :ns], z[:, ns:]
        for j, dist in enumerate((1, 2, 4)):
            ar = cf[2 * SUBLANES + 2 * j:2 * SUBLANES + 2 * j + 1, :]
            ai = cf[2 * SUBLANES + 2 * j + 1:2 * SUBLANES + 2 * j + 2, :]
            keep = rowmod >= dist
            sr = jnp.where(keep, pltpu.roll(zr, dist, 0), 0.0)
            si = jnp.where(keep, pltpu.roll(zi, dist, 0), 0.0)
            zr, zi = zr + ar * sr - ai * si, zi + ar * si + ai * sr
        er = jnp.where(rowmod >= 1, pltpu.roll(zr, 1, 0), 0.0)
        ei = jnp.where(rowmod >= 1, pltpu.roll(zi, 1, 0), 0.0)
        pr, pi = cf[0:SUBLANES, :], cf[SUBLANES:2 * SUBLANES, :]
        a8r, a8i = cf[2 * SUBLANES + 6:2 * SUBLANES + 7, :], cf[2 * SUBLANES + 7:2 * SUBLANES + 8, :]
        xr = jnp.zeros((1, ns), F32)
        xi = jnp.zeros((1, ns), F32)
        ent_r, ent_i = [], []
        for t in range(nc // SUBLANES):
            rows = slice(t * SUBLANES, (t + 1) * SUBLANES)
            ent_r.append(er[rows, :] + pr * xr - pi * xi)
            ent_i.append(ei[rows, :] + pr * xi + pi * xr)
            lr_, li_ = zr[(t + 1) * SUBLANES - 1:(t + 1) * SUBLANES, :], zi[(t + 1) * SUBLANES - 1:(t + 1) * SUBLANES, :]
            xr, xi = lr_ + a8r * xr - a8i * xi, li_ + a8r * xi + a8i * xr
        ent = jnp.concatenate([jnp.concatenate(ent_r, axis=0), jnp.concatenate(ent_i, axis=0)], axis=1)
        y = _dot3(u, tph_ref[0, qd], tpl_ref[0, qd]) + _dot3(ent, qeh_ref[0, qd], qel_ref[0, qd]) + d_ref[0, qd] * u
        ys.append(_gelu(y))
    for l in range(S5_L):
        y_ref[:, l, :] = jnp.concatenate([y[:, l * S5_PAIR:(l + 1) * S5_PAIR] for y in ys], axis=1)


def _s5(y3, ops, layer, bsz):
    rows = y3.shape[0]
    nc = rows // bsz
    pzh, pzl, qeh, qel, tph, tpl, coef, d2 = ops
    nblk = GROUP_WIDTH // LANES
    op = lambda a: pl.BlockSpec((1, S5_QUADS) + a.shape[2:], lambda s, b: (layer, s, 0, 0))
    return pl.pallas_call(
        _s5_kernel,
        grid=(nblk, bsz),
        in_specs=[pl.BlockSpec((nc, S5_L, LANES), lambda s, b: (b, 0, COL_S5 // LANES + s)),
                  op(pzh), op(pzl), op(qeh), op(qel), op(tph), op(tpl), op(coef), op(d2)],
        out_specs=pl.BlockSpec((nc, S5_L, LANES), lambda s, b: (b, 0, s)),
        out_shape=jax.ShapeDtypeStruct((rows, S5_L, GROUP_WIDTH), F32),
        compiler_params=_params(("parallel", "parallel")),
        name="s5",
    )(y3, pzh, pzl, qeh, qel, tph, tpl, coef, d2)


def _lru_kernel(x_ref, gate_ref, cw_ref, cb_ref, wa_ref, ba_ref, wx_ref, bx_ref, lp_ref, ng_ref, o_ref,
                xe_ref, a_s, v_s, h_ref):
    tb = x_ref.shape[0]
    first = pl.program_id(1) == 0

    @pl.when(first)
    def _():
        h_ref[...] = jnp.zeros_like(h_ref)

    xc = _causal_conv(x_ref[...], cw_ref, xe_ref, first) + cb_ref[...]
    r = _sigmoid(_dot(xc, wa_ref[...]) + ba_ref[...])
    gi = _sigmoid(_dot(xc, wx_ref[...]) + bx_ref[...])
    log_a = -LRU_C * r * _softplus(-lp_ref[...])
    a = jnp.exp(log_a)
    th = jnp.tanh(log_a)
    v = jnp.sqrt(-2.0 * th / (1.0 - th)) * gi * xc
    rowmod = lax.broadcasted_iota(jnp.int32, a.shape, 0) % SUBLANES
    for dist in (1, 2, 4):
        keep = rowmod >= dist
        a_sh = jnp.where(keep, pltpu.roll(a, dist, 0), 1.0)
        v_sh = jnp.where(keep, pltpu.roll(v, dist, 0), 0.0)
        v = v + a * v_sh
        a = a * a_sh
    a_s[...] = a
    v_s[...] = v
    h = h_ref[...]
    for t in range(tb // SUBLANES):
        rows = slice(t * SUBLANES, (t + 1) * SUBLANES)
        ht = v_s[rows, :] + a_s[rows, :] * h
        v_s[rows, :] = ht
        h = ht[SUBLANES - 1:SUBLANES, :]
    h_ref[...] = h
    o_ref[...] = _rms(v_s[...] * _gelu(gate_ref[...]), ng_ref[...])


def _lru(y, bsz, t_len, conv_w, conv_b, wa_bd, ba, wx_bd, bx, lparam, norm_g):
    tb = min(MIX_BLOCK, t_len)
    nb = t_len // tb
    w = GROUP_WIDTH
    const = lambda b, i: (0, 0)
    vec = pl.BlockSpec((1, w), const)
    return pl.pallas_call(
        _lru_kernel,
        grid=(bsz, nb),
        in_specs=[
            pl.BlockSpec((tb, w), lambda b, i: (b * nb + i, COL_LX // w)),
            pl.BlockSpec((tb, w), lambda b, i: (b * nb + i, COL_LG // w)),
            pl.BlockSpec((CONV_WIDTH, w), const), vec,
            pl.BlockSpec((w, w), const), vec,
            pl.BlockSpec((w, w), const), vec, vec, vec,
        ],
        out_specs=pl.BlockSpec((tb, w), lambda b, i: (b * nb + i, 0)),
        out_shape=jax.ShapeDtypeStruct((bsz * t_len, w), F32),
        scratch_shapes=[pltpu.VMEM((tb + SUBLANES, w), F32), pltpu.VMEM((tb, w), F32),
                        pltpu.VMEM((tb, w), F32), pltpu.VMEM((1, w), F32)],
        compiler_params=_params(("parallel", "arbitrary")),
        name="rglru",
    )(y, y, conv_w, conv_b, wa_bd, ba, wx_bd, bx, lparam, norm_g)


def _mix_out_kernel(x_ref, ya_ref, yb_ref, yc_ref, yd_ref, gw_ref, gb_ref, cg_ref, wo_ref, g_ref, b_ref, o_ref):
    w = GROUP_WIDTH
    yc = yc_ref[...]
    yc = yc * _sigmoid(_dot(yc, gw_ref[...]) + gb_ref[...])
    yc = _rms(yc, cg_ref[...])
    acc = _dot(ya_ref[...].astype(BF16), wo_ref[0:w, :])
    acc += _dot(yb_ref[...].astype(BF16), wo_ref[w:2 * w, :])
    acc += _dot(yc.astype(BF16), wo_ref[2 * w:3 * w, :])
    acc += _dot(yd_ref[...].astype(BF16), wo_ref[3 * w:4 * w, :])
    o_ref[...] = _layer_norm(DN_ALPHA * x_ref[...] + acc, g_ref[...], b_ref[...])


def _mix_out(x, ya, yb, yc, yd, glu_w, glu_b, c_norm_g, w_out, g, b, tm=256):
    n, d = x.shape
    w = GROUP_WIDTH
    tm = min(tm, n)
    const = lambda i: (0, 0)
    tok = lambda width: pl.BlockSpec((tm, width), lambda i: (i, 0))
    return pl.pallas_call(
        _mix_out_kernel,
        grid=(n // tm,),
        in_specs=[tok(d), tok(w), tok(w), tok(w), tok(w),
                  pl.BlockSpec((w, w), const), pl.BlockSpec((1, w), const), pl.BlockSpec((1, w), const),
                  pl.BlockSpec((d, d), const), pl.BlockSpec((1, d), const), pl.BlockSpec((1, d), const)],
        out_specs=tok(d),
        out_shape=jax.ShapeDtypeStruct((n, d), F32),
        compiler_params=_params(("parallel",)),
        name="mix_out",
    )(x, ya, yb, yc, yd, glu_w, glu_b, c_norm_g, w_out, g, b)


def _ple_kernel(x_ref, p_ref, wp_ref, wg_ref, g_ref, b_ref, o_ref):
    x = x_ref[...]
    gate = _sigmoid(_dot(x.astype(BF16), wg_ref[...]))
    ple = _dot(p_ref[...].astype(BF16), wp_ref[...]) * gate
    o_ref[...] = _layer_norm(DN_ALPHA * x + ple, g_ref[...], b_ref[...])


def _ple(x, p, wp, wg, g, b, tm=256):
    n, d = x.shape
    tm = min(tm, n)
    const = lambda i: (0, 0)
    return pl.pallas_call(
        _ple_kernel,
        grid=(n // tm,),
        in_specs=[pl.BlockSpec((tm, d), lambda i: (i, 0)), pl.BlockSpec((tm, p.shape[1]), lambda i: (i, 0)),
                  pl.BlockSpec(wp.shape, const), pl.BlockSpec(wg.shape, const),
                  pl.BlockSpec((1, d), const), pl.BlockSpec((1, d), const)],
        out_specs=pl.BlockSpec((tm, d), lambda i: (i, 0)),
        out_shape=jax.ShapeDtypeStruct((n, d), F32),
        compiler_params=_params(("parallel",)),
        name="ple",
    )(x, p, wp, wg, g, b)


def _block_diag(wb):
    nb, d, _ = wb.shape
    return jnp.einsum('nde,nm->ndme', wb, jnp.eye(nb, dtype=wb.dtype)).reshape(nb * d, nb * d)


def _pad_row(vals, offset):
    return jnp.zeros((1, LANES), F32).at[0, offset:offset + vals.shape[0]].set(vals)


def _token_mix(h, layer, bsz, t_len, w_in_p, w_out, gdn_conv_w, gdn_A_log, gdn_dt_bias, gdn_norm_g, lower_bound,
               hgrn_norm_g, s5_ops, s5_glu_w, s5_glu_b, lru_conv_w, lru_conv_b, lru_wa, lru_ba, lru_wx, lru_bx,
               lru_param, branch_norm_g, ln_g, ln_b, consts):
    w = GROUP_WIDTH
    n = bsz * t_len
    tri, m_all, pm = consts
    y = _proj(h, w_in_p)
    ya = _gdn(y, bsz, t_len, gdn_conv_w, _pad_row(gdn_A_log, N_HEADS), _pad_row(gdn_dt_bias, N_HEADS),
              gdn_norm_g.reshape(1, HEAD_DIM), tri)
    yb = _hgrn(y, bsz, t_len, lower_bound.reshape(1, w), hgrn_norm_g.reshape(1, HEAD_DIM), m_all, pm)
    yc = _s5(y.reshape(n // S5_L, S5_L, y.shape[1]), s5_ops, layer, bsz).reshape(n, w)
    yd = _lru(y, bsz, t_len, lru_conv_w, lru_conv_b.reshape(1, w), _block_diag(lru_wa), lru_ba.reshape(1, w),
              _block_diag(lru_wx), lru_bx.reshape(1, w), lru_param.reshape(1, w), branch_norm_g[1].reshape(1, w))
    return _mix_out(h, ya, yb, yc, yd, s5_glu_w, s5_glu_b.reshape(1, w), branch_norm_g[0].reshape(1, w),
                    w_out, ln_g, ln_b)


def kernel(x, p, ln_g, ln_b, ffn_wi, ffn_wo, mix_w_in, mix_w_out, gdn_conv_w, gdn_A_log, gdn_dt_bias, gdn_norm_g, hgrn_lb_logits, hgrn_norm_g, s5_lam_re, s5_lam_im, s5_log_dt, s5_B_re, s5_B_im, s5_C_re, s5_C_im, s5_D, s5_glu_w, s5_glu_b, lru_conv_w, lru_conv_b, lru_wa, lru_ba, lru_wx, lru_bx, lru_param, branch_norm_g, ple_w, ple_gate_w):
    bsz, t_len, d = x.shape
    n = bsz * t_len
    depth = ln_g.shape[0]
    lb_cum = jnp.cumsum(jax.nn.softmax(hgrn_lb_logits.astype(F32), axis=0), axis=0)
    lower_bounds = lb_cum - lb_cum[0:1]
    m_all, pm = _hgrn_masks()
    consts = (jnp.tril(jnp.ones((CHUNK, CHUNK), BF16)), jnp.asarray(m_all, BF16), jnp.asarray(pm))
    s5_ops = _s5_operators(s5_lam_re, s5_lam_im, s5_log_dt, s5_B_re, s5_B_im, s5_C_re, s5_C_im, s5_D)
    h = x.reshape(n, d)
    vec = lambda a: a.reshape(1, d)
    for i in range(depth):
        w_in = mix_w_in[i]
        w_in_p = jnp.concatenate(
            [w_in[:, :4 * GROUP_WIDTH], w_in[:, 4 * GROUP_WIDTH + 2 * N_HEADS:],
             w_in[:, 4 * GROUP_WIDTH:4 * GROUP_WIDTH + 2 * N_HEADS],
             jnp.zeros((d, IN_COLS_PAD - w_in.shape[1]), w_in.dtype)], axis=1).astype(BF16)
        h = _ffn(h, ffn_wi[i, 0].astype(BF16), ffn_wo[i, 0].astype(BF16), vec(ln_g[i, 0]), vec(ln_b[i, 0]))
        h = _token_mix(h, i, bsz, t_len, w_in_p, mix_w_out[i].astype(BF16), gdn_conv_w[i], gdn_A_log[i],
                       gdn_dt_bias[i], gdn_norm_g[i], lower_bounds[i], hgrn_norm_g[i], s5_ops, s5_glu_w[i],
                       s5_glu_b[i], lru_conv_w[i], lru_conv_b[i], lru_wa[i], lru_ba[i], lru_wx[i], lru_bx[i],
                       lru_param[i], branch_norm_g[i], vec(ln_g[i, 1]), vec(ln_b[i, 1]), consts)
        h = _ffn(h, ffn_wi[i, 1].astype(BF16), ffn_wo[i, 1].astype(BF16), vec(ln_g[i, 2]), vec(ln_b[i, 2]))
        h = _ple(h, p[i].reshape(n, p.shape[-1]), ple_w[i].astype(BF16), ple_gate_w[i].astype(BF16),
                 vec(ln_g[i, 3]), vec(ln_b[i, 3]))
    return h.reshape(bsz, t_len, d)
```

```python
import functools

import numpy as np
import jax
import jax.numpy as jnp
from jax import lax
from jax.experimental import pallas as pl
from jax.experimental.pallas import tpu as pltpu

F32 = jnp.float32
BF16 = jnp.bfloat16
HI = lax.Precision.HIGHEST

D_MODEL = 2048
DEPTH = 2
GROUP_WIDTH = D_MODEL // 4
HEAD_DIM = 128
N_HEADS = GROUP_WIDTH // HEAD_DIM
S5_CH = 16
S5_GROUPS = GROUP_WIDTH // S5_CH
S5_STATE = 64
LRU_BLOCKS = 8
LRU_C = 8.0
CONV_WIDTH = 4
D_FF = ((8 * D_MODEL // 3 + 255) // 256) * 256
PLE_DIM = 256
DN_ALPHA = (2 * DEPTH) ** 0.25
LN_EPS = 1e-5
RMS_EPS = 1e-6

SUBLANES = 8
LANES = 128
VMEM_LIMIT = 56 * 1024 * 1024

CHUNK = 64
MIX_BLOCK = 256
S5_L = 16
IN_COLS_PAD = 6144
COL_QKV, COL_Z, COL_HG, COL_S5, COL_LX, COL_LG, COL_BD = 0, 1536, 2048, 4096, 4608, 5120, 5632


def _params(sem):
    return pltpu.CompilerParams(dimension_semantics=sem, vmem_limit_bytes=VMEM_LIMIT)


def _sigmoid(x):
    return 1.0 / (1.0 + jnp.exp(-x))


def _silu(x):
    return x * _sigmoid(x)


def _softplus(x):
    return jnp.maximum(x, 0.0) + jnp.log(1.0 + jnp.exp(-jnp.abs(x)))


def _gelu(x):
    return 0.5 * x * (1.0 + jnp.tanh(0.7978845608028654 * (x + 0.044715 * x * x * x)))


def _layer_norm(y, g, b):
    mu = jnp.mean(y, axis=-1, keepdims=True)
    yc = y - mu
    var = jnp.mean(yc * yc, axis=-1, keepdims=True)
    return yc * lax.rsqrt(var + LN_EPS) * g + b


def _rms(y, g):
    return y * lax.rsqrt(jnp.mean(y * y, axis=-1, keepdims=True) + RMS_EPS) * g


def _dot(a, b, precision=None):
    return jnp.dot(a, b, preferred_element_type=F32, precision=precision)


def _dot_nt(a, b, precision=None):
    return lax.dot_general(a, b, (((1,), (1,)), ((), ())), preferred_element_type=F32, precision=precision)


def _dot_tn(a, b, precision=None):
    return lax.dot_general(a, b, (((0,), (0,)), ((), ())), preferred_element_type=F32, precision=precision)


def _bdot(a, b, ca, cb):
    return lax.dot_general(a, b, (((ca,), (cb,)), ((0,), (0,))), preferred_element_type=F32)


def _bf(x):
    return x.astype(BF16)


def _dot_split(m, x):
    x1 = x.astype(BF16)
    r1 = x - x1.astype(F32)
    x2 = r1.astype(BF16)
    x3 = (r1 - x2.astype(F32)).astype(BF16)
    return _dot(m, x3) + _dot(m, x2) + _dot(m, x1)


def _ffn_kernel(x_ref, wg_ref, wu_ref, wo_ref, g_ref, b_ref, o_ref, xb_ref, acc_ref):
    f = pl.program_id(1)

    @pl.when(f == 0)
    def _():
        xb_ref[...] = x_ref[...].astype(BF16)
        acc_ref[...] = jnp.zeros_like(acc_ref)

    xb = xb_ref[...]
    hg = _dot(xb, wg_ref[...])
    hu = _dot(xb, wu_ref[...])
    act = (_silu(hg) * hu).astype(BF16)
    acc_ref[...] += _dot(act, wo_ref[...])

    @pl.when(f == pl.num_programs(1) - 1)
    def _():
        y = DN_ALPHA * x_ref[...] + 0.5 * acc_ref[...]
        o_ref[...] = _layer_norm(y, g_ref[...], b_ref[...])


def _ffn(x, wi_all, wo_all, layer, sub, g, b, tm=512, tf=512):
    n, d = x.shape
    dff = wo_all.shape[2]
    tm = min(tm, n)
    nf = dff // tf
    return pl.pallas_call(
        _ffn_kernel,
        grid=(n // tm, nf),
        in_specs=[
            pl.BlockSpec((tm, d), lambda i, f: (i, 0)),
            pl.BlockSpec((None, None, d, tf), lambda i, f: (layer, sub, 0, f)),
            pl.BlockSpec((None, None, d, tf), lambda i, f: (layer, sub, 0, f + nf)),
            pl.BlockSpec((None, None, tf, d), lambda i, f: (layer, sub, f, 0)),
            pl.BlockSpec((1, d), lambda i, f: (0, 0)),
            pl.BlockSpec((1, d), lambda i, f: (0, 0)),
        ],
        out_specs=pl.BlockSpec((tm, d), lambda i, f: (i, 0)),
        out_shape=jax.ShapeDtypeStruct((n, d), F32),
        scratch_shapes=[pltpu.VMEM((tm, d), BF16), pltpu.VMEM((tm, d), F32)],
        compiler_params=_params(("parallel", "arbitrary")),
        name="ffn",
    )(x, wi_all, wi_all, wo_all, g, b)


def _proj_kernel(x_ref, w_ref, o_ref, xb_ref):
    @pl.when(pl.program_id(1) == 0)
    def _():
        xb_ref[...] = x_ref[...].astype(BF16)

    o_ref[...] = _dot(xb_ref[...], w_ref[...])


def _proj(x, w_all, layer, tm=1024, tn=1024):
    n, d = x.shape
    cols = w_all.shape[2]
    tm = min(tm, n)
    return pl.pallas_call(
        _proj_kernel,
        grid=(n // tm, cols // tn),
        in_specs=[pl.BlockSpec((tm, d), lambda i, j: (i, 0)),
                  pl.BlockSpec((None, d, tn), lambda i, j: (layer, 0, j))],
        out_specs=pl.BlockSpec((tm, tn), lambda i, j: (i, j)),
        out_shape=jax.ShapeDtypeStruct((n, cols), F32),
        scratch_shapes=[pltpu.VMEM((tm, d), BF16)],
        compiler_params=_params(("parallel", "arbitrary")),
        name="in_proj",
    )(x, w_all)


def _causal_conv(x, w_ref, xe_ref, first):
    tb = x.shape[0]

    @pl.when(first)
    def _():
        xe_ref[0:SUBLANES, :] = jnp.zeros((SUBLANES, x.shape[1]), F32)

    xe_ref[SUBLANES:SUBLANES + tb, :] = x
    acc = x * w_ref[CONV_WIDTH - 1:CONV_WIDTH, :]
    for j in range(CONV_WIDTH - 1):
        off = SUBLANES - (CONV_WIDTH - 1) + j
        acc = acc + xe_ref[off:off + tb, :] * w_ref[j:j + 1, :]
    xe_ref[0:SUBLANES, :] = xe_ref[tb:tb + SUBLANES, :]
    return acc


def _gdn_kernel(qkv_ref, z_ref, bd_ref, cw_ref, alog_ref, dtb_ref, ng_ref, tri_ref, o_ref, xe_ref, s_ref):
    tb = qkv_ref.shape[0]
    w = GROUP_WIDTH
    nch = tb // CHUNK
    first = pl.program_id(1) == 0

    @pl.when(first)
    def _():
        s_ref[...] = jnp.zeros_like(s_ref)

    qkv = _silu(_causal_conv(qkv_ref[...], cw_ref, xe_ref, first))
    bd = bd_ref[...]
    beta_all = _sigmoid(bd)
    g_all = -jnp.exp(alog_ref[...]) * _softplus(bd + dtb_ref[...])
    tri = tri_ref[...]

    def stack(col0):
        parts = []
        for ci in range(nch):
            for h in range(N_HEADS):
                parts.append(qkv[ci * CHUNK:(ci + 1) * CHUNK, col0 + h * HEAD_DIM:col0 + (h + 1) * HEAD_DIM][None])
        return jnp.concatenate(parts, axis=0)

    q, k, v = stack(0), stack(w), stack(2 * w)
    q = q * (lax.rsqrt(jnp.sum(q * q, axis=-1, keepdims=True) + RMS_EPS) * (HEAD_DIM ** -0.5))
    k = k * lax.rsqrt(jnp.sum(k * k, axis=-1, keepdims=True) + RMS_EPS)
    gcs, grs, bcs = [], [], []
    for ci in range(nch):
        rows = slice(ci * CHUNK, (ci + 1) * CHUNK)
        gcum = _dot_split(tri, g_all[rows, :])
        gcum_t = gcum.T
        for h in range(N_HEADS):
            gcs.append(gcum[:, N_HEADS + h:N_HEADS + h + 1][None])
            grs.append(gcum_t[N_HEADS + h:N_HEADS + h + 1, :][None])
            bcs.append(beta_all[rows, h:h + 1][None])
    gc = jnp.concatenate(gcs, axis=0)
    gr = jnp.concatenate(grs, axis=0)
    bc = jnp.concatenate(bcs, axis=0)

    r = lax.broadcasted_iota(jnp.int32, (1, CHUNK, CHUNK), 1)
    c = lax.broadcasted_iota(jnp.int32, (1, CHUNK, CHUNK), 2)
    eye = (r == c).astype(F32)
    dec = jnp.exp(jnp.where(r >= c, gc - gr, -jnp.inf))
    kb = _bf(k)
    a = bc * _bdot(kb, kb, 2, 2) * jnp.where(r > c, dec, 0.0)
    qk = _bdot(_bf(q), kb, 2, 2) * dec
    t = eye - a
    p = _bdot(_bf(a), _bf(a), 2, 1)
    n = 2
    while True:
        t = t + _bdot(_bf(t), _bf(p), 2, 1)
        n *= 2
        if n >= CHUNK:
            break
        p = _bdot(_bf(p), _bf(p), 2, 1)
    eg = jnp.exp(gc)
    rhs = jnp.concatenate([bc * v, (bc * eg) * k], axis=2)
    solb = _bf(rhs + _bdot(_bf(t - eye), _bf(rhs), 2, 1))
    glast = gc[:, CHUNK - 1:CHUNK, :]
    kd = k * jnp.exp(glast - gc)
    nk = _bdot(_bf(kd), solb, 1, 1)
    qw = _bdot(_bf(qk), solb, 2, 1)
    qeff = _bf(q * eg - qw[:, :, HEAD_DIM:])
    nku = nk[:, :, :HEAD_DIM]
    nkw = _bf(nk[:, :, HEAD_DIM:])
    gl = jnp.exp(glast)
    ng = ng_ref[...]

    s = s_ref[...]
    for ci in range(nch):
        sel = slice(ci * N_HEADS, (ci + 1) * N_HEADS)
        sb = _bf(s)
        o = qw[sel, :, :HEAD_DIM] + _bdot(qeff[sel], sb, 2, 1)
        s = gl[sel] * s + nku[sel] - _bdot(nkw[sel], sb, 2, 1)
        rows = slice(ci * CHUNK, (ci + 1) * CHUNK)
        for h in range(N_HEADS):
            hs = slice(h * HEAD_DIM, (h + 1) * HEAD_DIM)
            o_ref[rows, hs] = _rms(o[h], ng) * _silu(z_ref[rows, hs])
    s_ref[...] = s


def _gdn(y, bsz, t_len, conv_w, alog_row, dtb_row, norm_g, tri):
    tb = min(MIX_BLOCK, t_len)
    nb = t_len // tb
    w = GROUP_WIDTH
    row = lambda b, i: b * nb + i
    const = lambda b, i: (0, 0)
    return pl.pallas_call(
        _gdn_kernel,
        grid=(bsz, nb),
        in_specs=[
            pl.BlockSpec((tb, 3 * w), lambda b, i: (row(b, i), COL_QKV // (3 * w))),
            pl.BlockSpec((tb, w), lambda b, i: (row(b, i), COL_Z // w)),
            pl.BlockSpec((tb, LANES), lambda b, i: (row(b, i), COL_BD // LANES)),
            pl.BlockSpec((CONV_WIDTH, 3 * w), const),
            pl.BlockSpec((1, LANES), const),
            pl.BlockSpec((1, LANES), const),
            pl.BlockSpec((1, HEAD_DIM), const),
            pl.BlockSpec((CHUNK, CHUNK), const),
        ],
        out_specs=pl.BlockSpec((tb, w), lambda b, i: (row(b, i), 0)),
        out_shape=jax.ShapeDtypeStruct((bsz * t_len, w), F32),
        scratch_shapes=[pltpu.VMEM((tb + SUBLANES, 3 * w), F32),
                        pltpu.VMEM((N_HEADS, HEAD_DIM, HEAD_DIM), F32)],
        compiler_params=_params(("parallel", "arbitrary")),
        name="gdn",
    )(y, y, y, conv_w, alog_row, dtb_row, norm_g, tri)


def _hgrn_masks():
    c = CHUNK
    levels = int(np.log2(c))
    m = np.zeros((levels + 2, c, c), np.float32)
    pm = np.zeros((levels + 1, c, c), np.float32)
    for l in range(levels):
        n = c >> (l + 1)
        for row in range(c):
            base = (row // (2 * n)) * 2 * n
            mid = base + n
            if row >= mid:
                m[l, row, mid:row + 1] = 1.0
            else:
                m[l, row, row + 1:mid] = 1.0
        for t in range(c):
            for s in range(c):
                if t // (2 * n) == s // (2 * n) and t % (2 * n) >= n and s % (2 * n) < n:
                    pm[l, t, s] = 1.0
    pm[levels] = np.eye(c, dtype=np.float32)
    m[levels] = np.tril(np.ones((c, c), np.float32))
    m[levels + 1] = np.triu(np.ones((c, c), np.float32), 1)
    return m.reshape((levels + 2) * c, c), pm


def _hgrn_kernel(x_ref, lb_ref, ng_ref, m_ref, pm_ref, o_ref, st_ref):
    tb = x_ref.shape[0]
    w = GROUP_WIDTH
    nch = tb // CHUNK
    levels = pm_ref.shape[0] - 1

    @pl.when(pl.program_id(1) == 0)
    def _():
        st_ref[...] = jnp.zeros_like(st_ref)

    lb = lb_ref[...]
    ng = ng_ref[...]
    m_all = m_ref[...]
    fl = x_ref[:, w:2 * w]
    e = jnp.exp(-jnp.abs(fl))
    rcp = 1.0 / (1.0 + e)
    sig = jnp.where(fl >= 0, rcp, e * rcp)
    nsig = jnp.where(fl >= 0, e * rcp, rcp)
    log_f = jnp.log(lb + (1.0 - lb) * sig)
    kk = (1.0 - lb) * nsig
    e_all = [jnp.exp(_dot_split(m_all, log_f[ci * CHUNK:(ci + 1) * CHUNK, :])) for ci in range(nch)]

    def stack(get):
        return jnp.concatenate([get(ci, h)[None] for ci in range(nch) for h in range(N_HEADS)], axis=0)

    blk = lambda a, ci, h: a[ci * CHUNK:(ci + 1) * CHUNK, h * HEAD_DIM:(h + 1) * HEAD_DIM]
    lvl = lambda l: stack(lambda ci, h: e_all[ci][l * CHUNK:(l + 1) * CHUNK, h * HEAD_DIM:(h + 1) * HEAD_DIM])
    q = stack(lambda ci, h: x_ref[ci * CHUNK:(ci + 1) * CHUNK, h * HEAD_DIM:(h + 1) * HEAD_DIM]) * (HEAD_DIM ** -0.5)
    k = stack(lambda ci, h: blk(kk, ci, h))
    vb = _bf(stack(lambda ci, h: x_ref[ci * CHUNK:(ci + 1) * CHUNK, 2 * w + h * HEAD_DIM:2 * w + (h + 1) * HEAD_DIM]))
    att = _bdot(_bf(q), _bf(k), 2, 2) * pm_ref[levels][None]
    for l in range(levels):
        el = lvl(l)
        att = att + _bdot(_bf(q * el), _bf(k * el), 2, 2) * pm_ref[l][None]
    eb = lvl(levels)
    er = lvl(levels + 1)
    o_intra = _bdot(_bf(att), vb, 2, 1)
    upd = _bdot(vb, _bf(k * er), 1, 1)
    qe = _bf(q * eb)
    dlast = eb[:, CHUNK - 1:CHUNK, :]

    st = st_ref[...]
    for ci in range(nch):
        sel = slice(ci * N_HEADS, (ci + 1) * N_HEADS)
        o = o_intra[sel] + _bdot(qe[sel], _bf(st), 2, 2)
        st = st * dlast[sel] + upd[sel]
        rows = slice(ci * CHUNK, (ci + 1) * CHUNK)
        for h in range(N_HEADS):
            cs = slice(h * HEAD_DIM, (h + 1) * HEAD_DIM)
            gg = x_ref[rows, 3 * w + h * HEAD_DIM:3 * w + (h + 1) * HEAD_DIM]
            o_ref[rows, cs] = _rms(o[h], ng) * _silu(gg)
    st_ref[...] = st


def _hgrn(y, bsz, t_len, lb_row, norm_g, m_all, pm):
    tb = min(MIX_BLOCK, t_len)
    nb = t_len // tb
    w = GROUP_WIDTH
    const2 = lambda b, i: (0, 0)
    return pl.pallas_call(
        _hgrn_kernel,
        grid=(bsz, nb),
        in_specs=[
            pl.BlockSpec((tb, 4 * w), lambda b, i: (b * nb + i, COL_HG // (4 * w))),
            pl.BlockSpec((1, w), const2),
            pl.BlockSpec((1, HEAD_DIM), const2),
            pl.BlockSpec(m_all.shape, const2),
            pl.BlockSpec(pm.shape, lambda b, i: (0, 0, 0)),
        ],
        out_specs=pl.BlockSpec((tb, w), lambda b, i: (b * nb + i, 0)),
        out_shape=jax.ShapeDtypeStruct((bsz * t_len, w), F32),
        scratch_shapes=[pltpu.VMEM((N_HEADS, HEAD_DIM, HEAD_DIM), F32)],
        compiler_params=_params(("parallel", "arbitrary")),
        name="hgrn2",
    )(y, lb_row, norm_g, m_all, pm)


S5_PAIR = 2 * S5_CH
S5_PAIRS = S5_GROUPS // 2
S5_FEAT = S5_L * S5_PAIR
S5_NS = 2 * S5_STATE
S5_QUADS = LANES // S5_PAIR


def _hi_lo(a):
    hi = a.astype(BF16)
    return hi, (a - hi.astype(F32)).astype(BF16)


def _toeplitz_kernel(k_ref, hi_ref, lo_ref):
    zero = jnp.zeros((S5_PAIR, S5_PAIR), F32)
    for s in range(S5_L):
        row = jnp.concatenate([zero] * s + [k_ref[0, tau] for tau in range(S5_L - s)], axis=1)
        hi = row.astype(BF16)
        hi_ref[0, s * S5_PAIR:(s + 1) * S5_PAIR, :] = hi
        lo_ref[0, s * S5_PAIR:(s + 1) * S5_PAIR, :] = (row - hi.astype(F32)).astype(BF16)


def _s5_toeplitz(k2):
    nprob = k2.shape[0]
    out = jax.ShapeDtypeStruct((nprob, S5_FEAT, S5_FEAT), BF16)
    return pl.pallas_call(
        _toeplitz_kernel,
        grid=(nprob,),
        in_specs=[pl.BlockSpec((1,) + k2.shape[1:], lambda i: (i, 0, 0, 0))],
        out_specs=[pl.BlockSpec((1, S5_FEAT, S5_FEAT), lambda i: (i, 0, 0))] * 2,
        out_shape=[out, out],
        compiler_params=_params(("parallel",)),
        name="s5_toeplitz",
    )(k2)


def _s5_operators(lam_re, lam_im, log_dt, b_re, b_im, c_re, c_im, d_skip):
    dep = lam_re.shape[0]
    n, p, el, gp = S5_STATE, S5_CH, S5_L, S5_PAIRS
    mm = functools.partial(jnp.einsum, precision=HI)
    dt = jnp.exp(log_dt)[..., None]
    lr, li = lam_re * dt, lam_im * dt
    mag = jnp.exp(lr)
    ab_re, ab_im = mag * jnp.cos(li), mag * jnp.sin(li)
    den = lam_re * lam_re + lam_im * lam_im
    nr = ab_re - 1.0
    cc_re = (nr * lam_re + ab_im * lam_im) / den
    cc_im = (ab_im * lam_re - nr * lam_im) / den
    bb_re = cc_re[..., None] * b_re - cc_im[..., None] * b_im
    bb_im = cc_re[..., None] * b_im + cc_im[..., None] * b_re
    ks = jnp.arange(el + 1, dtype=F32)[:, None, None, None]
    pw_re = jnp.exp(ks * lr[None]) * jnp.cos(ks * li[None])
    pw_im = jnp.exp(ks * lr[None]) * jnp.sin(ks * li[None])
    rp_re, rp_im = pw_re[el - 1::-1], pw_im[el - 1::-1]
    p_re = rp_re[..., None] * bb_re[None] - rp_im[..., None] * bb_im[None]
    p_im = rp_re[..., None] * bb_im[None] + rp_im[..., None] * bb_re[None]
    fp_re, fp_im = pw_re[1:el + 1, :, :, None, :], pw_im[1:el + 1, :, :, None, :]
    q_re = c_re[None] * fp_re - c_im[None] * fp_im
    q_im = c_re[None] * fp_im + c_im[None] * fp_re
    cp_re = c_re[None] * pw_re[:el, :, :, None, :] - c_im[None] * pw_im[:el, :, :, None, :]
    cp_im = c_re[None] * pw_im[:el, :, :, None, :] + c_im[None] * pw_re[:el, :, :, None, :]
    kern = mm('tdgpn,dgnq->tdgpq', cp_re, bb_re) - mm('tdgpn,dgnq->tdgpq', cp_im, bb_im)
    eye2 = jnp.eye(2, dtype=F32)
    k2 = jnp.einsum('tdagpq,gh->datgqhp', kern.reshape(el, dep, gp, 2, p, p), eye2)
    tp_hi, tp_lo = _s5_toeplitz(k2.reshape(dep * gp, el, S5_PAIR, S5_PAIR))
    tp_hi = tp_hi.reshape(dep, gp, S5_FEAT, S5_FEAT)
    tp_lo = tp_lo.reshape(dep, gp, S5_FEAT, S5_FEAT)
    pz = jnp.concatenate(
        [jnp.einsum('ldagnp,gh->dalgphn', t.reshape(el, dep, gp, 2, n, p), eye2).reshape(dep, gp, S5_FEAT, S5_NS)
         for t in (p_re, p_im)], axis=-1)
    qe = jnp.concatenate(
        [jnp.einsum('ldagpn,gh->dagnlhp', t.reshape(el, dep, gp, 2, p, n), eye2).reshape(dep, gp, S5_NS, S5_FEAT)
         for t in (q_re, -q_im)], axis=-2)
    cmul = lambda x, y: (x[0] * y[0] - x[1] * y[1], x[0] * y[1] + x[1] * y[0])
    a1 = (pw_re[el], pw_im[el])
    a2 = cmul(a1, a1)
    a4 = cmul(a2, a2)
    a3 = cmul(a2, a1)
    pows = [(jnp.ones_like(a1[0]), jnp.zeros_like(a1[0])), a1, a2, a3, a4, cmul(a4, a1), cmul(a4, a2), cmul(a4, a3)]
    a8 = cmul(a4, a4)
    rows = [t[0] for t in pows] + [t[1] for t in pows] + [a1[0], a1[1], a2[0], a2[1], a4[0], a4[1], a8[0], a8[1]]
    coef = jnp.stack([t.reshape(dep, gp, S5_NS) for t in rows], axis=2)
    d2 = jnp.tile(d_skip.reshape(dep, gp, 1, S5_PAIR), (1, 1, el, 1)).reshape(dep, gp, 1, S5_FEAT)
    return _hi_lo(pz) + _hi_lo(qe) + (tp_hi, tp_lo, coef, d2)


def _dot3(x, w_hi, w_lo):
    x_hi = x.astype(BF16)
    x_lo = (x - x_hi.astype(F32)).astype(BF16)
    return _dot(x_lo, w_hi) + _dot(x_hi, w_lo) + _dot(x_hi, w_hi)


def _s5_kernel(u_ref, pzh_ref, pzl_ref, qeh_ref, qel_ref, tph_ref, tpl_ref, cf_ref, d_ref, y_ref):
    nc = u_ref.shape[0]
    ns = S5_NS
    tok = [u_ref[:, l, :] for l in range(S5_L)]
    rowmod = lax.broadcasted_iota(jnp.int32, (nc, ns), 0) % SUBLANES
    ys = []
    for qd in range(S5_QUADS):
        lanes = slice(qd * S5_PAIR, (qd + 1) * S5_PAIR)
        u = jnp.concatenate([t[:, lanes] for t in tok], axis=1)
        cf = cf_ref[0, qd]
        z = _dot3(u, pzh_ref[0, qd], pzl_ref[0, qd])
        zr, zi = z[:, :ns], z[:, ns:]
        for j, dist in enumerate((1, 2, 4)):
            ar = cf[2 * SUBLANES + 2 * j:2 * SUBLANES + 2 * j + 1, :]
            ai = cf[2 * SUBLANES + 2 * j + 1:2 * SUBLANES + 2 * j + 2, :]
            keep = rowmod >= dist
            sr = jnp.where(keep, pltpu.roll(zr, dist, 0), 0.0)
            si = jnp.where(keep, pltpu.roll(zi, dist, 0), 0.0)
            zr, zi = zr + ar * sr - ai * si, zi + ar * si + ai * sr
        er = jnp.where(rowmod >= 1, pltpu.roll(zr, 1, 0), 0.0)
        ei = jnp.where(rowmod >= 1, pltpu.roll(zi, 1, 0), 0.0)
        pr, pi = cf[0:SUBLANES, :], cf[SUBLANES:2 * SUBLANES, :]
        a8r, a8i = cf[2 * SUBLANES + 6:2 * SUBLANES + 7, :], cf[2 * SUBLANES + 7:2 * SUBLANES + 8, :]
        xr = jnp.zeros((1, ns), F32)
        xi = jnp.zeros((1, ns), F32)
        ent_r, ent_i = [], []
        for t in range(nc // SUBLANES):
            rows = slice(t * SUBLANES, (t + 1) * SUBLANES)
            ent_r.append(er[rows, :] + pr * xr - pi * xi)
            ent_i.append(ei[rows, :] + pr * xi + pi * xr)
            lr_, li_ = zr[(t + 1) * SUBLANES - 1:(t + 1) * SUBLANES, :], zi[(t + 1) * SUBLANES - 1:(t + 1) * SUBLANES, :]
            xr, xi = lr_ + a8r * xr - a8i * xi, li_ + a8r * xi + a8i * xr
        ent = jnp.concatenate([jnp.concatenate(ent_r, axis=0), jnp.concatenate(ent_i, axis=0)], axis=1)
        y = _dot3(u, tph_ref[0, qd], tpl_ref[0, qd]) + _dot3(ent, qeh_ref[0, qd], qel_ref[0, qd]) + d_ref[0, qd] * u
        ys.append(_gelu(y))
    for l in range(S5_L):
        y_ref[:, l, :] = jnp.concatenate([y[:, l * S5_PAIR:(l + 1) * S5_PAIR] for y in ys], axis=1)


def _s5(y3, ops, layer, bsz):
    rows = y3.shape[0]
    nc = rows // bsz
    pzh, pzl, qeh, qel, tph, tpl, coef, d2 = ops
    nblk = GROUP_WIDTH // LANES
    op = lambda a: pl.BlockSpec((1, S5_QUADS) + a.shape[2:], lambda s, b: (layer, s, 0, 0))
    return pl.pallas_call(
        _s5_kernel,
        grid=(nblk, bsz),
        in_specs=[pl.BlockSpec((nc, S5_L, LANES), lambda s, b: (b, 0, COL_S5 // LANES + s)),
                  op(pzh), op(pzl), op(qeh), op(qel), op(tph), op(tpl), op(coef), op(d2)],
        out_specs=pl.BlockSpec((nc, S5_L, LANES), lambda s, b: (b, 0, s)),
        out_shape=jax.ShapeDtypeStruct((rows, S5_L, GROUP_WIDTH), F32),
        compiler_params=_params(("parallel", "parallel")),
        name="s5",
    )(y3, pzh, pzl, qeh, qel, tph, tpl, coef, d2)


def _lru_kernel(x_ref, gate_ref, cw_ref, cb_ref, wa_ref, ba_ref, wx_ref, bx_ref, lp_ref, ng_ref, o_ref,
                xe_ref, a_s, v_s, h_ref):
    tb = x_ref.shape[0]
    first = pl.program_id(1) == 0

    @pl.when(first)
    def _():
        h_ref[...] = jnp.zeros_like(h_ref)

    xc = _causal_conv(x_ref[...], cw_ref, xe_ref, first) + cb_ref[...]
    r = _sigmoid(_dot(xc, wa_ref[...]) + ba_ref[...])
    gi = _sigmoid(_dot(xc, wx_ref[...]) + bx_ref[...])
    log_a = -LRU_C * r * _softplus(-lp_ref[...])
    a = jnp.exp(log_a)
    th = jnp.tanh(log_a)
    v = jnp.sqrt(-2.0 * th / (1.0 - th)) * gi * xc
    rowmod = lax.broadcasted_iota(jnp.int32, a.shape, 0) % SUBLANES
    for dist in (1, 2, 4):
        keep = rowmod >= dist
        a_sh = jnp.where(keep, pltpu.roll(a, dist, 0), 1.0)
        v_sh = jnp.where(keep, pltpu.roll(v, dist, 0), 0.0)
        v = v + a * v_sh
        a = a * a_sh
    a_s[...] = a
    v_s[...] = v
    h = h_ref[...]
    for t in range(tb // SUBLANES):
        rows = slice(t * SUBLANES, (t + 1) * SUBLANES)
        ht = v_s[rows, :] + a_s[rows, :] * h
        v_s[rows, :] = ht
        h = ht[SUBLANES - 1:SUBLANES, :]
    h_ref[...] = h
    o_ref[...] = _rms(v_s[...] * _gelu(gate_ref[...]), ng_ref[...])


def _lru(y, bsz, t_len, conv_w, conv_b, wa_bd, ba, wx_bd, bx, lparam, norm_g):
    tb = min(MIX_BLOCK, t_len)
    nb = t_len // tb
    w = GROUP_WIDTH
    const = lambda b, i: (0, 0)
    vec = pl.BlockSpec((1, w), const)
    return pl.pallas_call(
        _lru_kernel,
        grid=(bsz, nb),
        in_specs=[
            pl.BlockSpec((tb, w), lambda b, i: (b * nb + i, COL_LX // w)),
            pl.BlockSpec((tb, w), lambda b, i: (b * nb + i, COL_LG // w)),
            pl.BlockSpec((CONV_WIDTH, w), const), vec,
            pl.BlockSpec((w, w), const), vec,
            pl.BlockSpec((w, w), const), vec, vec, vec,
        ],
        out_specs=pl.BlockSpec((tb, w), lambda b, i: (b * nb + i, 0)),
        out_shape=jax.ShapeDtypeStruct((bsz * t_len, w), F32),
        scratch_shapes=[pltpu.VMEM((tb + SUBLANES, w), F32), pltpu.VMEM((tb, w), F32),
                        pltpu.VMEM((tb, w), F32), pltpu.VMEM((1, w), F32)],
        compiler_params=_params(("parallel", "arbitrary")),
        name="rglru",
    )(y, y, conv_w, conv_b, wa_bd, ba, wx_bd, bx, lparam, norm_g)


def _mix_out_kernel(x_ref, ya_ref, yb_ref, yc_ref, yd_ref, gw_ref, gb_ref, cg_ref, wo_ref, g_ref, b_ref, o_ref):
    w = GROUP_WIDTH
    yc = yc_ref[...]
    yc = yc * _sigmoid(_dot(yc, gw_ref[...]) + gb_ref[...])
    yc = _rms(yc, cg_ref[...])
    acc = _dot(ya_ref[...].astype(BF16), wo_ref[0:w, :])
    acc += _dot(yb_ref[...].astype(BF16), wo_ref[w:2 * w, :])
    acc += _dot(yc.astype(BF16), wo_ref[2 * w:3 * w, :])
    acc += _dot(yd_ref[...].astype(BF16), wo_ref[3 * w:4 * w, :])
    o_ref[...] = _layer_norm(DN_ALPHA * x_ref[...] + acc, g_ref[...], b_ref[...])


def _mix_out(x, ya, yb, yc, yd, glu_w, glu_b, c_norm_g, w_out, g, b, tm=256):
    n, d = x.shape
    w = GROUP_WIDTH
    tm = min(tm, n)
    const = lambda i: (0, 0)
    tok = lambda width: pl.BlockSpec((tm, width), lambda i: (i, 0))
    return pl.pallas_call(
        _mix_out_kernel,
        grid=(n // tm,),
        in_specs=[tok(d), tok(w), tok(w), tok(w), tok(w),
                  pl.BlockSpec((w, w), const), pl.BlockSpec((1, w), const), pl.BlockSpec((1, w), const),
                  pl.BlockSpec((d, d), const), pl.BlockSpec((1, d), const), pl.BlockSpec((1, d), const)],
        out_specs=tok(d),
        out_shape=jax.ShapeDtypeStruct((n, d), F32),
        compiler_params=_params(("parallel",)),
        name="mix_out",
    )(x, ya, yb, yc, yd, glu_w, glu_b, c_norm_g, w_out, g, b)


def _ple_kernel(x_ref, p_ref, wp_ref, wg_ref, g_ref, b_ref, o_ref):
    x = x_ref[...]
    gate = _sigmoid(_dot(x.astype(BF16), wg_ref[...]))
    ple = _dot(p_ref[...].astype(BF16), wp_ref[...]) * gate
    o_ref[...] = _layer_norm(DN_ALPHA * x + ple, g_ref[...], b_ref[...])


def _ple(x, p, wp, wg, g, b, tm=256):
    n, d = x.shape
    tm = min(tm, n)
    const = lambda i: (0, 0)
    return pl.pallas_call(
        _ple_kernel,
        grid=(n // tm,),
        in_specs=[pl.BlockSpec((tm, d), lambda i: (i, 0)), pl.BlockSpec((tm, p.shape[1]), lambda i: (i, 0)),
                  pl.BlockSpec(wp.shape, const), pl.BlockSpec(wg.shape, const),
                  pl.BlockSpec((1, d), const), pl.BlockSpec((1, d), const)],
        out_specs=pl.BlockSpec((tm, d), lambda i: (i, 0)),
        out_shape=jax.ShapeDtypeStruct((n, d), F32),
        compiler_params=_params(("parallel",)),
        name="ple",
    )(x, p, wp, wg, g, b)


def _block_diag(wb):
    nb, d, _ = wb.shape
    return jnp.einsum('nde,nm->ndme', wb, jnp.eye(nb, dtype=wb.dtype)).reshape(nb * d, nb * d)


def _pad_row(vals, offset):
    return jnp.zeros((1, LANES), F32).at[0, offset:offset + vals.shape[0]].set(vals)


def _token_mix(h, layer, bsz, t_len, w_in_p, w_out, gdn_conv_w, gdn_A_log, gdn_dt_bias, gdn_norm_g, lower_bound,
               hgrn_norm_g, s5_ops, s5_glu_w, s5_glu_b, lru_conv_w, lru_conv_b, lru_wa, lru_ba, lru_wx, lru_bx,
               lru_param, branch_norm_g, ln_g, ln_b, consts):
    w = GROUP_WIDTH
    n = bsz * t_len
    tri, m_all, pm = consts
    y = _proj(h, w_in_p, layer)
    ya = _gdn(y, bsz, t_len, gdn_conv_w, _pad_row(gdn_A_log, N_HEADS), _pad_row(gdn_dt_bias, N_HEADS),
              gdn_norm_g.reshape(1, HEAD_DIM), tri)
    yb = _hgrn(y, bsz, t_len, lower_bound.reshape(1, w), hgrn_norm_g.reshape(1, HEAD_DIM), m_all, pm)
    yc = _s5(y.reshape(n // S5_L, S5_L, y.shape[1]), s5_ops, layer, bsz).reshape(n, w)
    yd = _lru(y, bsz, t_len, lru_conv_w, lru_conv_b.reshape(1, w), _block_diag(lru_wa), lru_ba.reshape(1, w),
              _block_diag(lru_wx), lru_bx.reshape(1, w), lru_param.reshape(1, w), branch_norm_g[1].reshape(1, w))
    return _mix_out(h, ya, yb, yc, yd, s5_glu_w, s5_glu_b.reshape(1, w), branch_norm_g[0].reshape(1, w),
                    w_out, ln_g, ln_b)


def kernel(x, p, ln_g, ln_b, ffn_wi, ffn_wo, mix_w_in, mix_w_out, gdn_conv_w, gdn_A_log, gdn_dt_bias, gdn_norm_g, hgrn_lb_logits, hgrn_norm_g, s5_lam_re, s5_lam_im, s5_log_dt, s5_B_re, s5_B_im, s5_C_re, s5_C_im, s5_D, s5_glu_w, s5_glu_b, lru_conv_w, lru_conv_b, lru_wa, lru_ba, lru_wx, lru_bx, lru_param, branch_norm_g, ple_w, ple_gate_w):
    bsz, t_len, d = x.shape
    n = bsz * t_len
    depth = ln_g.shape[0]
    lb_cum = jnp.cumsum(jax.nn.softmax(hgrn_lb_logits.astype(F32), axis=0), axis=0)
    lower_bounds = lb_cum - lb_cum[0:1]
    m_all, pm = _hgrn_masks()
    consts = (jnp.tril(jnp.ones((CHUNK, CHUNK), BF16)), jnp.asarray(m_all, BF16), jnp.asarray(pm))
    s5_ops = _s5_operators(s5_lam_re, s5_lam_im, s5_log_dt, s5_B_re, s5_B_im, s5_C_re, s5_C_im, s5_D)
    h = x.reshape(n, d)
    vec = lambda a: a.reshape(1, d)
    nbd = 2 * N_HEADS
    w_in_p = jnp.concatenate(
        [mix_w_in[:, :, :4 * GROUP_WIDTH], mix_w_in[:, :, 4 * GROUP_WIDTH + nbd:],
         mix_w_in[:, :, 4 * GROUP_WIDTH:4 * GROUP_WIDTH + nbd],
         jnp.zeros((depth, d, IN_COLS_PAD - mix_w_in.shape[2]), mix_w_in.dtype)], axis=2).astype(BF16)
    wi_all = ffn_wi.astype(BF16)
    wo_all = ffn_wo.astype(BF16)
    for i in range(depth):
        h = _ffn(h, wi_all, wo_all, i, 0, vec(ln_g[i, 0]), vec(ln_b[i, 0]))
        h = _token_mix(h, i, bsz, t_len, w_in_p, mix_w_out[i].astype(BF16), gdn_conv_w[i], gdn_A_log[i],
                       gdn_dt_bias[i], gdn_norm_g[i], lower_bounds[i], hgrn_norm_g[i], s5_ops, s5_glu_w[i],
                       s5_glu_b[i], lru_conv_w[i], lru_conv_b[i], lru_wa[i], lru_ba[i], lru_wx[i], lru_bx[i],
                       lru_param[i], branch_norm_g[i], vec(ln_g[i, 1]), vec(ln_b[i, 1]), consts)
        h = _ffn(h, wi_all, wo_all, i, 1, vec(ln_g[i, 2]), vec(ln_b[i, 2]))
        h = _ple(h, p[i].reshape(n, p.shape[-1]), ple_w[i].astype(BF16), ple_gate_w[i].astype(BF16),
                 vec(ln_g[i, 3]), vec(ln_b[i, 3]))
    return h.reshape(bsz, t_len, d)
```

```python
import numpy as np
import jax
import jax.numpy as jnp
from jax import lax
from jax.experimental import pallas as pl
from jax.experimental.pallas import tpu as pltpu

F32 = jnp.float32
BF16 = jnp.bfloat16
HI = lax.Precision.HIGHEST

D_MODEL = 2048
DEPTH = 2
GROUP_WIDTH = D_MODEL // 4
HEAD_DIM = 128
N_HEADS = GROUP_WIDTH // HEAD_DIM
S5_CH = 16
S5_GROUPS = GROUP_WIDTH // S5_CH
S5_STATE = 64
LRU_BLOCKS = 8
LRU_C = 8.0
CONV_WIDTH = 4
D_FF = ((8 * D_MODEL // 3 + 255) // 256) * 256
PLE_DIM = 256
DN_ALPHA = (2 * DEPTH) ** 0.25
LN_EPS = 1e-5
RMS_EPS = 1e-6

SUBLANES = 8
LANES = 128
VMEM_LIMIT = 56 * 1024 * 1024

CHUNK = 64
MIX_BLOCK = 256
S5_L = 16
IN_COLS_PAD = 6144
COL_QKV, COL_Z, COL_HG, COL_S5, COL_LX, COL_LG, COL_BD = 0, 1536, 2048, 4096, 4608, 5120, 5632


def _params(sem):
    return pltpu.CompilerParams(dimension_semantics=sem, vmem_limit_bytes=VMEM_LIMIT)


def _sigmoid(x):
    return 1.0 / (1.0 + jnp.exp(-x))


def _silu(x):
    return x * _sigmoid(x)


def _softplus(x):
    return jnp.maximum(x, 0.0) + jnp.log(1.0 + jnp.exp(-jnp.abs(x)))


def _gelu(x):
    return 0.5 * x * (1.0 + jnp.tanh(0.7978845608028654 * (x + 0.044715 * x * x * x)))


def _layer_norm(y, g, b):
    mu = jnp.mean(y, axis=-1, keepdims=True)
    yc = y - mu
    var = jnp.mean(yc * yc, axis=-1, keepdims=True)
    return yc * lax.rsqrt(var + LN_EPS) * g + b


def _rms(y, g):
    return y * lax.rsqrt(jnp.mean(y * y, axis=-1, keepdims=True) + RMS_EPS) * g


def _dot(a, b, precision=None):
    return jnp.dot(a, b, preferred_element_type=F32, precision=precision)


def _dot_nt(a, b, precision=None):
    return lax.dot_general(a, b, (((1,), (1,)), ((), ())), preferred_element_type=F32, precision=precision)


def _dot_tn(a, b, precision=None):
    return lax.dot_general(a, b, (((0,), (0,)), ((), ())), preferred_element_type=F32, precision=precision)


def _bdot(a, b, ca, cb):
    return lax.dot_general(a, b, (((ca,), (cb,)), ((0,), (0,))), preferred_element_type=F32)


def _bf(x):
    return x.astype(BF16)


def _dot_split(m, x):
    x1 = x.astype(BF16)
    r1 = x - x1.astype(F32)
    x2 = r1.astype(BF16)
    x3 = (r1 - x2.astype(F32)).astype(BF16)
    return _dot(m, x3) + _dot(m, x2) + _dot(m, x1)


def _ffn_kernel(x_ref, wg_ref, wu_ref, wo_ref, g_ref, b_ref, o_ref, xb_ref, acc_ref):
    f = pl.program_id(1)

    @pl.when(f == 0)
    def _():
        xb_ref[...] = x_ref[...].astype(BF16)
        acc_ref[...] = jnp.zeros_like(acc_ref)

    xb = xb_ref[...]
    hg = _dot(xb, wg_ref[...])
    hu = _dot(xb, wu_ref[...])
    act = (_silu(hg) * hu).astype(BF16)
    acc_ref[...] += _dot(act, wo_ref[...])

    @pl.when(f == pl.num_programs(1) - 1)
    def _():
        y = DN_ALPHA * x_ref[...] + 0.5 * acc_ref[...]
        o_ref[...] = _layer_norm(y, g_ref[...], b_ref[...])


def _ffn(x, wi_all, wo_all, layer, sub, g, b, tm=512, tf=512):
    n, d = x.shape
    dff = wo_all.shape[2]
    tm = min(tm, n)
    nf = dff // tf
    return pl.pallas_call(
        _ffn_kernel,
        grid=(n // tm, nf),
        in_specs=[
            pl.BlockSpec((tm, d), lambda i, f: (i, 0)),
            pl.BlockSpec((None, None, d, tf), lambda i, f: (layer, sub, 0, f)),
            pl.BlockSpec((None, None, d, tf), lambda i, f: (layer, sub, 0, f + nf)),
            pl.BlockSpec((None, None, tf, d), lambda i, f: (layer, sub, f, 0)),
            pl.BlockSpec((1, d), lambda i, f: (0, 0)),
            pl.BlockSpec((1, d), lambda i, f: (0, 0)),
        ],
        out_specs=pl.BlockSpec((tm, d), lambda i, f: (i, 0)),
        out_shape=jax.ShapeDtypeStruct((n, d), F32),
        scratch_shapes=[pltpu.VMEM((tm, d), BF16), pltpu.VMEM((tm, d), F32)],
        compiler_params=_params(("parallel", "arbitrary")),
        name="ffn",
    )(x, wi_all, wi_all, wo_all, g, b)


def _proj_kernel(x_ref, w_ref, o_ref, xb_ref):
    @pl.when(pl.program_id(1) == 0)
    def _():
        xb_ref[...] = x_ref[...].astype(BF16)

    o_ref[...] = _dot(xb_ref[...], w_ref[...])


def _proj(x, w_all, layer, tm=1024, tn=1024):
    n, d = x.shape
    cols = w_all.shape[2]
    tm = min(tm, n)
    return pl.pallas_call(
        _proj_kernel,
        grid=(n // tm, cols // tn),
        in_specs=[pl.BlockSpec((tm, d), lambda i, j: (i, 0)),
                  pl.BlockSpec((None, d, tn), lambda i, j: (layer, 0, j))],
        out_specs=pl.BlockSpec((tm, tn), lambda i, j: (i, j)),
        out_shape=jax.ShapeDtypeStruct((n, cols), F32),
        scratch_shapes=[pltpu.VMEM((tm, d), BF16)],
        compiler_params=_params(("parallel", "arbitrary")),
        name="in_proj",
    )(x, w_all)


def _causal_conv(x, w_ref, xe_ref, first):
    tb = x.shape[0]

    @pl.when(first)
    def _():
        xe_ref[0:SUBLANES, :] = jnp.zeros((SUBLANES, x.shape[1]), F32)

    xe_ref[SUBLANES:SUBLANES + tb, :] = x
    acc = x * w_ref[CONV_WIDTH - 1:CONV_WIDTH, :]
    for j in range(CONV_WIDTH - 1):
        off = SUBLANES - (CONV_WIDTH - 1) + j
        acc = acc + xe_ref[off:off + tb, :] * w_ref[j:j + 1, :]
    xe_ref[0:SUBLANES, :] = xe_ref[tb:tb + SUBLANES, :]
    return acc


def _gdn_kernel(qkv_ref, z_ref, bd_ref, cw_ref, alog_ref, dtb_ref, ng_ref, tri_ref, o_ref, xe_ref, s_ref):
    tb = qkv_ref.shape[0]
    w = GROUP_WIDTH
    nch = tb // CHUNK
    first = pl.program_id(1) == 0

    @pl.when(first)
    def _():
        s_ref[...] = jnp.zeros_like(s_ref)

    qkv = _silu(_causal_conv(qkv_ref[...], cw_ref, xe_ref, first))
    bd = bd_ref[...]
    beta_all = _sigmoid(bd)
    g_all = -jnp.exp(alog_ref[...]) * _softplus(bd + dtb_ref[...])
    tri = tri_ref[...]

    def stack(col0):
        parts = []
        for ci in range(nch):
            for h in range(N_HEADS):
                parts.append(qkv[ci * CHUNK:(ci + 1) * CHUNK, col0 + h * HEAD_DIM:col0 + (h + 1) * HEAD_DIM][None])
        return jnp.concatenate(parts, axis=0)

    q, k, v = stack(0), stack(w), stack(2 * w)
    q = q * (lax.rsqrt(jnp.sum(q * q, axis=-1, keepdims=True) + RMS_EPS) * (HEAD_DIM ** -0.5))
    k = k * lax.rsqrt(jnp.sum(k * k, axis=-1, keepdims=True) + RMS_EPS)
    gcs, grs, bcs = [], [], []
    for ci in range(nch):
        rows = slice(ci * CHUNK, (ci + 1) * CHUNK)
        gcum = _dot_split(tri, g_all[rows, :])
        gcum_t = gcum.T
        for h in range(N_HEADS):
            gcs.append(gcum[:, N_HEADS + h:N_HEADS + h + 1][None])
            grs.append(gcum_t[N_HEADS + h:N_HEADS + h + 1, :][None])
            bcs.append(beta_all[rows, h:h + 1][None])
    gc = jnp.concatenate(gcs, axis=0)
    gr = jnp.concatenate(grs, axis=0)
    bc = jnp.concatenate(bcs, axis=0)

    r = lax.broadcasted_iota(jnp.int32, (1, CHUNK, CHUNK), 1)
    c = lax.broadcasted_iota(jnp.int32, (1, CHUNK, CHUNK), 2)
    eye = (r == c).astype(F32)
    dec = jnp.exp(jnp.where(r >= c, gc - gr, -jnp.inf))
    kb = _bf(k)
    a = bc * _bdot(kb, kb, 2, 2) * jnp.where(r > c, dec, 0.0)
    qk = _bdot(_bf(q), kb, 2, 2) * dec
    t = eye - a
    p = _bdot(_bf(a), _bf(a), 2, 1)
    n = 2
    while True:
        t = t + _bdot(_bf(t), _bf(p), 2, 1)
        n *= 2
        if n >= CHUNK:
            break
        p = _bdot(_bf(p), _bf(p), 2, 1)
    eg = jnp.exp(gc)
    rhs = jnp.concatenate([bc * v, (bc * eg) * k], axis=2)
    solb = _bf(rhs + _bdot(_bf(t - eye), _bf(rhs), 2, 1))
    glast = gc[:, CHUNK - 1:CHUNK, :]
    kd = k * jnp.exp(glast - gc)
    nk = _bdot(_bf(kd), solb, 1, 1)
    qw = _bdot(_bf(qk), solb, 2, 1)
    qeff = _bf(q * eg - qw[:, :, HEAD_DIM:])
    nku = nk[:, :, :HEAD_DIM]
    nkw = _bf(nk[:, :, HEAD_DIM:])
    gl = jnp.exp(glast)
    ng = ng_ref[...]

    s = s_ref[...]
    for ci in range(nch):
        sel = slice(ci * N_HEADS, (ci + 1) * N_HEADS)
        sb = _bf(s)
        o = qw[sel, :, :HEAD_DIM] + _bdot(qeff[sel], sb, 2, 1)
        s = gl[sel] * s + nku[sel] - _bdot(nkw[sel], sb, 2, 1)
        rows = slice(ci * CHUNK, (ci + 1) * CHUNK)
        for h in range(N_HEADS):
            hs = slice(h * HEAD_DIM, (h + 1) * HEAD_DIM)
            o_ref[rows, hs] = _rms(o[h], ng) * _silu(z_ref[rows, hs])
    s_ref[...] = s


def _gdn(y, bsz, t_len, conv_w, alog_row, dtb_row, norm_g, tri):
    tb = min(MIX_BLOCK, t_len)
    nb = t_len // tb
    w = GROUP_WIDTH
    row = lambda b, i: b * nb + i
    const = lambda b, i: (0, 0)
    return pl.pallas_call(
        _gdn_kernel,
        grid=(bsz, nb),
        in_specs=[
            pl.BlockSpec((tb, 3 * w), lambda b, i: (row(b, i), COL_QKV // (3 * w))),
            pl.BlockSpec((tb, w), lambda b, i: (row(b, i), COL_Z // w)),
            pl.BlockSpec((tb, LANES), lambda b, i: (row(b, i), COL_BD // LANES)),
            pl.BlockSpec((CONV_WIDTH, 3 * w), const),
            pl.BlockSpec((1, LANES), const),
            pl.BlockSpec((1, LANES), const),
            pl.BlockSpec((1, HEAD_DIM), const),
            pl.BlockSpec((CHUNK, CHUNK), const),
        ],
        out_specs=pl.BlockSpec((tb, w), lambda b, i: (row(b, i), 0)),
        out_shape=jax.ShapeDtypeStruct((bsz * t_len, w), F32),
        scratch_shapes=[pltpu.VMEM((tb + SUBLANES, 3 * w), F32),
                        pltpu.VMEM((N_HEADS, HEAD_DIM, HEAD_DIM), F32)],
        compiler_params=_params(("parallel", "arbitrary")),
        name="gdn",
    )(y, y, y, conv_w, alog_row, dtb_row, norm_g, tri)


def _hgrn_masks():
    c = CHUNK
    levels = int(np.log2(c))
    m = np.zeros((levels + 2, c, c), np.float32)
    pm = np.zeros((levels + 1, c, c), np.float32)
    for l in range(levels):
        n = c >> (l + 1)
        for row in range(c):
            base = (row // (2 * n)) * 2 * n
            mid = base + n
            if row >= mid:
                m[l, row, mid:row + 1] = 1.0
            else:
                m[l, row, row + 1:mid] = 1.0
        for t in range(c):
            for s in range(c):
                if t // (2 * n) == s // (2 * n) and t % (2 * n) >= n and s % (2 * n) < n:
                    pm[l, t, s] = 1.0
    pm[levels] = np.eye(c, dtype=np.float32)
    m[levels] = np.tril(np.ones((c, c), np.float32))
    m[levels + 1] = np.triu(np.ones((c, c), np.float32), 1)
    return m.reshape((levels + 2) * c, c), pm


def _hgrn_kernel(x_ref, lb_ref, ng_ref, m_ref, pm_ref, o_ref, st_ref):
    tb = x_ref.shape[0]
    w = GROUP_WIDTH
    nch = tb // CHUNK
    levels = pm_ref.shape[0] - 1

    @pl.when(pl.program_id(1) == 0)
    def _():
        st_ref[...] = jnp.zeros_like(st_ref)

    lb = lb_ref[...]
    ng = ng_ref[...]
    m_all = m_ref[...]
    fl = x_ref[:, w:2 * w]
    e = jnp.exp(-jnp.abs(fl))
    rcp = 1.0 / (1.0 + e)
    sig = jnp.where(fl >= 0, rcp, e * rcp)
    nsig = jnp.where(fl >= 0, e * rcp, rcp)
    log_f = jnp.log(lb + (1.0 - lb) * sig)
    kk = (1.0 - lb) * nsig
    e_all = [jnp.exp(_dot_split(m_all, log_f[ci * CHUNK:(ci + 1) * CHUNK, :])) for ci in range(nch)]

    def stack(get):
        return jnp.concatenate([get(ci, h)[None] for ci in range(nch) for h in range(N_HEADS)], axis=0)

    blk = lambda a, ci, h: a[ci * CHUNK:(ci + 1) * CHUNK, h * HEAD_DIM:(h + 1) * HEAD_DIM]
    lvl = lambda l: stack(lambda ci, h: e_all[ci][l * CHUNK:(l + 1) * CHUNK, h * HEAD_DIM:(h + 1) * HEAD_DIM])
    q = stack(lambda ci, h: x_ref[ci * CHUNK:(ci + 1) * CHUNK, h * HEAD_DIM:(h + 1) * HEAD_DIM]) * (HEAD_DIM ** -0.5)
    k = stack(lambda ci, h: blk(kk, ci, h))
    vb = _bf(stack(lambda ci, h: x_ref[ci * CHUNK:(ci + 1) * CHUNK, 2 * w + h * HEAD_DIM:2 * w + (h + 1) * HEAD_DIM]))
    att = _bdot(_bf(q), _bf(k), 2, 2) * pm_ref[levels][None]
    for l in range(levels):
        el = lvl(l)
        att = att + _bdot(_bf(q * el), _bf(k * el), 2, 2) * pm_ref[l][None]
    eb = lvl(levels)
    er = lvl(levels + 1)
    o_intra = _bdot(_bf(att), vb, 2, 1)
    upd = _bdot(vb, _bf(k * er), 1, 1)
    qe = _bf(q * eb)
    dlast = eb[:, CHUNK - 1:CHUNK, :]

    st = st_ref[...]
    for ci in range(nch):
        sel = slice(ci * N_HEADS, (ci + 1) * N_HEADS)
        o = o_intra[sel] + _bdot(qe[sel], _bf(st), 2, 2)
        st = st * dlast[sel] + upd[sel]
        rows = slice(ci * CHUNK, (ci + 1) * CHUNK)
        for h in range(N_HEADS):
            cs = slice(h * HEAD_DIM, (h + 1) * HEAD_DIM)
            gg = x_ref[rows, 3 * w + h * HEAD_DIM:3 * w + (h + 1) * HEAD_DIM]
            o_ref[rows, cs] = _rms(o[h], ng) * _silu(gg)
    st_ref[...] = st


def _hgrn(y, bsz, t_len, lb_row, norm_g, m_all, pm):
    tb = min(MIX_BLOCK, t_len)
    nb = t_len // tb
    w = GROUP_WIDTH
    const2 = lambda b, i: (0, 0)
    return pl.pallas_call(
        _hgrn_kernel,
        grid=(bsz, nb),
        in_specs=[
            pl.BlockSpec((tb, 4 * w), lambda b, i: (b * nb + i, COL_HG // (4 * w))),
            pl.BlockSpec((1, w), const2),
            pl.BlockSpec((1, HEAD_DIM), const2),
            pl.BlockSpec(m_all.shape, const2),
            pl.BlockSpec(pm.shape, lambda b, i: (0, 0, 0)),
        ],
        out_specs=pl.BlockSpec((tb, w), lambda b, i: (b * nb + i, 0)),
        out_shape=jax.ShapeDtypeStruct((bsz * t_len, w), F32),
        scratch_shapes=[pltpu.VMEM((N_HEADS, HEAD_DIM, HEAD_DIM), F32)],
        compiler_params=_params(("parallel", "arbitrary")),
        name="hgrn2",
    )(y, lb_row, norm_g, m_all, pm)


S5_PAIR = 2 * S5_CH
S5_PAIRS = S5_GROUPS // 2
S5_FEAT = S5_L * S5_PAIR
S5_NS = 2 * S5_STATE
S5_QUADS = LANES // S5_PAIR


S5_KP = 24


def _s5_prep_kernel(bre_ref, bim_ref, ctre_ref, ctim_ref, cre_ref, cim_ref, pw_ref, pwt_ref,
                    pzh_ref, pzl_ref, qeh_ref, qel_ref, tph_ref, tpl_ref):
    el = S5_L
    bre, bim = bre_ref[0], bim_ref[0]
    ctre, ctim = ctre_ref[0], ctim_ref[0]
    cre, cim = cre_ref[0], cim_ref[0]
    pw = pw_ref[0]
    pwt = pwt_ref[0]

    def put(hi_ref, lo_ref, rows, val):
        hi = val.astype(BF16)
        hi_ref[0, rows, :] = hi
        lo_ref[0, rows, :] = (val - hi.astype(F32)).astype(BF16)

    def b_times_power(k):
        lr_, li_ = pw[k:k + 1, :], pw[S5_KP + k:S5_KP + k + 1, :]
        return bre * lr_ - bim * li_, bre * li_ + bim * lr_

    kerns = []
    for tau in range(el):
        xr, xi = b_times_power(tau)
        kerns.append(_dot_nt(xr, ctre, HI) - _dot_nt(xi, ctim, HI))
    zero = jnp.zeros((S5_PAIR, S5_PAIR), F32)
    for s in range(el):
        put(tph_ref, tpl_ref, slice(s * S5_PAIR, (s + 1) * S5_PAIR),
            jnp.concatenate([zero] * s + kerns[:el - s], axis=1))
    for l in range(el):
        xr, xi = b_times_power(el - 1 - l)
        put(pzh_ref, pzl_ref, slice(l * S5_PAIR, (l + 1) * S5_PAIR), jnp.concatenate([xr, xi], axis=1))
    tops, bots = [], []
    for l in range(el):
        cr_, ci_ = pwt[:, l + 1:l + 2], pwt[:, S5_KP + l + 1:S5_KP + l + 2]
        tops.append(cre * cr_ - cim * ci_)
        bots.append(-(cre * ci_ + cim * cr_))
    put(qeh_ref, qel_ref, slice(0, S5_NS), jnp.concatenate(tops, axis=1))
    put(qeh_ref, qel_ref, slice(S5_NS, 2 * S5_NS), jnp.concatenate(bots, axis=1))


def _s5_operators(lam_re, lam_im, log_dt, b_re, b_im, c_re, c_im, d_skip):
    dep = lam_re.shape[0]
    n, p, el, gp = S5_STATE, S5_CH, S5_L, S5_PAIRS
    dt = jnp.exp(log_dt)[..., None]
    lr, li = lam_re * dt, lam_im * dt
    mag = jnp.exp(lr)
    ab_re, ab_im = mag * jnp.cos(li), mag * jnp.sin(li)
    den = lam_re * lam_re + lam_im * lam_im
    nr = ab_re - 1.0
    cc_re = (nr * lam_re + ab_im * lam_im) / den
    cc_im = (ab_im * lam_re - nr * lam_im) / den
    bb_re = cc_re[..., None] * b_re - cc_im[..., None] * b_im
    bb_im = cc_re[..., None] * b_im + cc_im[..., None] * b_re
    ks = jnp.arange(S5_KP, dtype=F32)[:, None, None, None]
    pw_re = jnp.exp(ks * lr[None]) * jnp.cos(ks * li[None])
    pw_im = jnp.exp(ks * lr[None]) * jnp.sin(ks * li[None])
    eye2 = jnp.eye(2, dtype=F32)
    npair = dep * gp
    bt = [jnp.einsum('dagnq,gh->dagqhn', t.reshape(dep, gp, 2, n, p), eye2).reshape(npair, S5_PAIR, S5_NS)
          for t in (bb_re, bb_im)]
    ct = [jnp.einsum('dahpn,gh->dahpgn', t.reshape(dep, gp, 2, p, n), eye2).reshape(npair, S5_PAIR, S5_NS)
          for t in (c_re, c_im)]
    cc = [jnp.swapaxes(t, 1, 2) for t in ct]
    pw = jnp.concatenate([jnp.moveaxis(t.reshape(S5_KP, dep, gp, 2, n), 0, 2).reshape(npair, S5_KP, S5_NS)
                          for t in (pw_re, pw_im)], axis=1)
    pwt = jnp.swapaxes(pw, 1, 2)
    blk = lambda a: pl.BlockSpec((1,) + a.shape[1:], lambda i: (i, 0, 0))
    outs = [jax.ShapeDtypeStruct((npair, S5_FEAT, 2 * S5_NS), BF16)] * 2 \
        + [jax.ShapeDtypeStruct((npair, 2 * S5_NS, S5_FEAT), BF16)] * 2 \
        + [jax.ShapeDtypeStruct((npair, S5_FEAT, S5_FEAT), BF16)] * 2
    args = bt + ct + cc + [pw, pwt]
    ops = pl.pallas_call(
        _s5_prep_kernel,
        grid=(npair,),
        in_specs=[blk(a) for a in args],
        out_specs=[blk(o) for o in outs],
        out_shape=outs,
        compiler_params=_params(("parallel",)),
        name="s5_prep",
    )(*args)
    ops = [o.reshape((dep, gp) + o.shape[1:]) for o in ops]
    cmul = lambda x, y: (x[0] * y[0] - x[1] * y[1], x[0] * y[1] + x[1] * y[0])
    a1 = (pw_re[el], pw_im[el])
    a2 = cmul(a1, a1)
    a4 = cmul(a2, a2)
    a3 = cmul(a2, a1)
    pows = [(jnp.ones_like(a1[0]), jnp.zeros_like(a1[0])), a1, a2, a3, a4, cmul(a4, a1), cmul(a4, a2), cmul(a4, a3)]
    a8 = cmul(a4, a4)
    rows = [t[0] for t in pows] + [t[1] for t in pows] + [a1[0], a1[1], a2[0], a2[1], a4[0], a4[1], a8[0], a8[1]]
    coef = jnp.stack([t.reshape(dep, gp, S5_NS) for t in rows], axis=2)
    d2 = jnp.tile(d_skip.reshape(dep, gp, 1, S5_PAIR), (1, 1, el, 1)).reshape(dep, gp, 1, S5_FEAT)
    return tuple(ops) + (coef, d2)


def _dot3(x, w_hi, w_lo):
    x_hi = x.astype(BF16)
    x_lo = (x - x_hi.astype(F32)).astype(BF16)
    return _dot(x_lo, w_hi) + _dot(x_hi, w_lo) + _dot(x_hi, w_hi)


def _s5_kernel(u_ref, pzh_ref, pzl_ref, qeh_ref, qel_ref, tph_ref, tpl_ref, cf_ref, d_ref, y_ref):
    nc = u_ref.shape[0]
    ns = S5_NS
    tok = [u_ref[:, l, :] for l in range(S5_L)]
    rowmod = lax.broadcasted_iota(jnp.int32, (nc, ns), 0) % SUBLANES
    ys = []
    for qd in range(S5_QUADS):
        lanes = slice(qd * S5_PAIR, (qd + 1) * S5_PAIR)
        u = jnp.concatenate([t[:, lanes] for t in tok], axis=1)
        cf = cf_ref[0, qd]
        z = _dot3(u, pzh_ref[0, qd], pzl_ref[0, qd])
        zr, zi = z[:, :ns], z[:, ns:]
        for j, dist in enumerate((1, 2, 4)):
            ar = cf[2 * SUBLANES + 2 * j:2 * SUBLANES + 2 * j + 1, :]
            ai = cf[2 * SUBLANES + 2 * j + 1:2 * SUBLANES + 2 * j + 2, :]
            keep = rowmod >= dist
            sr = jnp.where(keep, pltpu.roll(zr, dist, 0), 0.0)
            si = jnp.where(keep, pltpu.roll(zi, dist, 0), 0.0)
            zr, zi = zr + ar * sr - ai * si, zi + ar * si + ai * sr
        er = jnp.where(rowmod >= 1, pltpu.roll(zr, 1, 0), 0.0)
        ei = jnp.where(rowmod >= 1, pltpu.roll(zi, 1, 0), 0.0)
        pr, pi = cf[0:SUBLANES, :], cf[SUBLANES:2 * SUBLANES, :]
        a8r, a8i = cf[2 * SUBLANES + 6:2 * SUBLANES + 7, :], cf[2 * SUBLANES + 7:2 * SUBLANES + 8, :]
        xr = jnp.zeros((1, ns), F32)
        xi = jnp.zeros((1, ns), F32)
        ent_r, ent_i = [], []
        for t in range(nc // SUBLANES):
            rows = slice(t * SUBLANES, (t + 1) * SUBLANES)
            ent_r.append(er[rows, :] + pr * xr - pi * xi)
            ent_i.append(ei[rows, :] + pr * xi + pi * xr)
            lr_, li_ = zr[(t + 1) * SUBLANES - 1:(t + 1) * SUBLANES, :], zi[(t + 1) * SUBLANES - 1:(t + 1) * SUBLANES, :]
            xr, xi = lr_ + a8r * xr - a8i * xi, li_ + a8r * xi + a8i * xr
        ent = jnp.concatenate([jnp.concatenate(ent_r, axis=0), jnp.concatenate(ent_i, axis=0)], axis=1)
        y = _dot3(u, tph_ref[0, qd], tpl_ref[0, qd]) + _dot3(ent, qeh_ref[0, qd], qel_ref[0, qd]) + d_ref[0, qd] * u
        ys.append(_gelu(y))
    for l in range(S5_L):
        y_ref[:, l, :] = jnp.concatenate([y[:, l * S5_PAIR:(l + 1) * S5_PAIR] for y in ys], axis=1)


def _s5(y3, ops, layer, bsz):
    rows = y3.shape[0]
    nc = rows // bsz
    pzh, pzl, qeh, qel, tph, tpl, coef, d2 = ops
    nblk = GROUP_WIDTH // LANES
    op = lambda a: pl.BlockSpec((1, S5_QUADS) + a.shape[2:], lambda s, b: (layer, s, 0, 0))
    return pl.pallas_call(
        _s5_kernel,
        grid=(nblk, bsz),
        in_specs=[pl.BlockSpec((nc, S5_L, LANES), lambda s, b: (b, 0, COL_S5 // LANES + s)),
                  op(pzh), op(pzl), op(qeh), op(qel), op(tph), op(tpl), op(coef), op(d2)],
        out_specs=pl.BlockSpec((nc, S5_L, LANES), lambda s, b: (b, 0, s)),
        out_shape=jax.ShapeDtypeStruct((rows, S5_L, GROUP_WIDTH), F32),
        compiler_params=_params(("parallel", "parallel")),
        name="s5",
    )(y3, pzh, pzl, qeh, qel, tph, tpl, coef, d2)


def _lru_kernel(x_ref, gate_ref, cw_ref, cb_ref, wa_ref, ba_ref, wx_ref, bx_ref, lp_ref, ng_ref, o_ref,
                xe_ref, a_s, v_s, h_ref):
    tb = x_ref.shape[0]
    first = pl.program_id(1) == 0

    @pl.when(first)
    def _():
        h_ref[...] = jnp.zeros_like(h_ref)

    xc = _causal_conv(x_ref[...], cw_ref, xe_ref, first) + cb_ref[...]
    r = _sigmoid(_dot(xc, wa_ref[...]) + ba_ref[...])
    gi = _sigmoid(_dot(xc, wx_ref[...]) + bx_ref[...])
    log_a = -LRU_C * r * _softplus(-lp_ref[...])
    a = jnp.exp(log_a)
    th = jnp.tanh(log_a)
    v = jnp.sqrt(-2.0 * th / (1.0 - th)) * gi * xc
    rowmod = lax.broadcasted_iota(jnp.int32, a.shape, 0) % SUBLANES
    for dist in (1, 2, 4):
        keep = rowmod >= dist
        a_sh = jnp.where(keep, pltpu.roll(a, dist, 0), 1.0)
        v_sh = jnp.where(keep, pltpu.roll(v, dist, 0), 0.0)
        v = v + a * v_sh
        a = a * a_sh
    a_s[...] = a
    v_s[...] = v
    h = h_ref[...]
    for t in range(tb // SUBLANES):
        rows = slice(t * SUBLANES, (t + 1) * SUBLANES)
        ht = v_s[rows, :] + a_s[rows, :] * h
        v_s[rows, :] = ht
        h = ht[SUBLANES - 1:SUBLANES, :]
    h_ref[...] = h
    o_ref[...] = _rms(v_s[...] * _gelu(gate_ref[...]), ng_ref[...])


def _lru(y, bsz, t_len, conv_w, conv_b, wa_bd, ba, wx_bd, bx, lparam, norm_g):
    tb = min(MIX_BLOCK, t_len)
    nb = t_len // tb
    w = GROUP_WIDTH
    const = lambda b, i: (0, 0)
    vec = pl.BlockSpec((1, w), const)
    return pl.pallas_call(
        _lru_kernel,
        grid=(bsz, nb),
        in_specs=[
            pl.BlockSpec((tb, w), lambda b, i: (b * nb + i, COL_LX // w)),
            pl.BlockSpec((tb, w), lambda b, i: (b * nb + i, COL_LG // w)),
            pl.BlockSpec((CONV_WIDTH, w), const), vec,
            pl.BlockSpec((w, w), const), vec,
            pl.BlockSpec((w, w), const), vec, vec, vec,
        ],
        out_specs=pl.BlockSpec((tb, w), lambda b, i: (b * nb + i, 0)),
        out_shape=jax.ShapeDtypeStruct((bsz * t_len, w), F32),
        scratch_shapes=[pltpu.VMEM((tb + SUBLANES, w), F32), pltpu.VMEM((tb, w), F32),
                        pltpu.VMEM((tb, w), F32), pltpu.VMEM((1, w), F32)],
        compiler_params=_params(("parallel", "arbitrary")),
        name="rglru",
    )(y, y, conv_w, conv_b, wa_bd, ba, wx_bd, bx, lparam, norm_g)


def _mix_out_kernel(x_ref, ya_ref, yb_ref, yc_ref, yd_ref, gw_ref, gb_ref, cg_ref, wo_ref, g_ref, b_ref, o_ref):
    w = GROUP_WIDTH
    yc = yc_ref[...]
    yc = yc * _sigmoid(_dot(yc, gw_ref[...]) + gb_ref[...])
    yc = _rms(yc, cg_ref[...])
    acc = _dot(ya_ref[...].astype(BF16), wo_ref[0:w, :])
    acc += _dot(yb_ref[...].astype(BF16), wo_ref[w:2 * w, :])
    acc += _dot(yc.astype(BF16), wo_ref[2 * w:3 * w, :])
    acc += _dot(yd_ref[...].astype(BF16), wo_ref[3 * w:4 * w, :])
    o_ref[...] = _layer_norm(DN_ALPHA * x_ref[...] + acc, g_ref[...], b_ref[...])


def _mix_out(x, ya, yb, yc, yd, glu_w, glu_b, c_norm_g, w_out, g, b, tm=256):
    n, d = x.shape
    w = GROUP_WIDTH
    tm = min(tm, n)
    const = lambda i: (0, 0)
    tok = lambda width: pl.BlockSpec((tm, width), lambda i: (i, 0))
    return pl.pallas_call(
        _mix_out_kernel,
        grid=(n // tm,),
        in_specs=[tok(d), tok(w), tok(w), tok(w), tok(w),
                  pl.BlockSpec((w, w), const), pl.BlockSpec((1, w), const), pl.BlockSpec((1, w), const),
                  pl.BlockSpec((d, d), const), pl.BlockSpec((1, d), const), pl.BlockSpec((1, d), const)],
        out_specs=tok(d),
        out_shape=jax.ShapeDtypeStruct((n, d), F32),
        compiler_params=_params(("parallel",)),
        name="mix_out",
    )(x, ya, yb, yc, yd, glu_w, glu_b, c_norm_g, w_out, g, b)


def _ple_kernel(x_ref, p_ref, wp_ref, wg_ref, g_ref, b_ref, o_ref):
    x = x_ref[...]
    gate = _sigmoid(_dot(x.astype(BF16), wg_ref[...]))
    ple = _dot(p_ref[...].astype(BF16), wp_ref[...]) * gate
    o_ref[...] = _layer_norm(DN_ALPHA * x + ple, g_ref[...], b_ref[...])


def _ple(x, p, wp, wg, g, b, tm=256):
    n, d = x.shape
    tm = min(tm, n)
    const = lambda i: (0, 0)
    return pl.pallas_call(
        _ple_kernel,
        grid=(n // tm,),
        in_specs=[pl.BlockSpec((tm, d), lambda i: (i, 0)), pl.BlockSpec((tm, p.shape[1]), lambda i: (i, 0)),
                  pl.BlockSpec(wp.shape, const), pl.BlockSpec(wg.shape, const),
                  pl.BlockSpec((1, d), const), pl.BlockSpec((1, d), const)],
        out_specs=pl.BlockSpec((tm, d), lambda i: (i, 0)),
        out_shape=jax.ShapeDtypeStruct((n, d), F32),
        compiler_params=_params(("parallel",)),
        name="ple",
    )(x, p, wp, wg, g, b)


def _block_diag(wb):
    nb, d, _ = wb.shape
    return jnp.einsum('nde,nm->ndme', wb, jnp.eye(nb, dtype=wb.dtype)).reshape(nb * d, nb * d)


def _pad_row(vals, offset):
    return jnp.zeros((1, LANES), F32).at[0, offset:offset + vals.shape[0]].set(vals)


def _token_mix(h, layer, bsz, t_len, w_in_p, w_out, gdn_conv_w, gdn_A_log, gdn_dt_bias, gdn_norm_g, lower_bound,
               hgrn_norm_g, s5_ops, s5_glu_w, s5_glu_b, lru_conv_w, lru_conv_b, lru_wa, lru_ba, lru_wx, lru_bx,
               lru_param, branch_norm_g, ln_g, ln_b, consts):
    w = GROUP_WIDTH
    n = bsz * t_len
    tri, m_all, pm = consts
    y = _proj(h, w_in_p, layer)
    ya = _gdn(y, bsz, t_len, gdn_conv_w, _pad_row(gdn_A_log, N_HEADS), _pad_row(gdn_dt_bias, N_HEADS),
              gdn_norm_g.reshape(1, HEAD_DIM), tri)
    yb = _hgrn(y, bsz, t_len, lower_bound.reshape(1, w), hgrn_norm_g.reshape(1, HEAD_DIM), m_all, pm)
    yc = _s5(y.reshape(n // S5_L, S5_L, y.shape[1]), s5_ops, layer, bsz).reshape(n, w)
    yd = _lru(y, bsz, t_len, lru_conv_w, lru_conv_b.reshape(1, w), _block_diag(lru_wa), lru_ba.reshape(1, w),
              _block_diag(lru_wx), lru_bx.reshape(1, w), lru_param.reshape(1, w), branch_norm_g[1].reshape(1, w))
    return _mix_out(h, ya, yb, yc, yd, s5_glu_w, s5_glu_b.reshape(1, w), branch_norm_g[0].reshape(1, w),
                    w_out, ln_g, ln_b)


def kernel(x, p, ln_g, ln_b, ffn_wi, ffn_wo, mix_w_in, mix_w_out, gdn_conv_w, gdn_A_log, gdn_dt_bias, gdn_norm_g, hgrn_lb_logits, hgrn_norm_g, s5_lam_re, s5_lam_im, s5_log_dt, s5_B_re, s5_B_im, s5_C_re, s5_C_im, s5_D, s5_glu_w, s5_glu_b, lru_conv_w, lru_conv_b, lru_wa, lru_ba, lru_wx, lru_bx, lru_param, branch_norm_g, ple_w, ple_gate_w):
    bsz, t_len, d = x.shape
    n = bsz * t_len
    depth = ln_g.shape[0]
    lb_cum = jnp.cumsum(jax.nn.softmax(hgrn_lb_logits.astype(F32), axis=0), axis=0)
    lower_bounds = lb_cum - lb_cum[0:1]
    m_all, pm = _hgrn_masks()
    consts = (jnp.tril(jnp.ones((CHUNK, CHUNK), BF16)), jnp.asarray(m_all, BF16), jnp.asarray(pm))
    s5_ops = _s5_operators(s5_lam_re, s5_lam_im, s5_log_dt, s5_B_re, s5_B_im, s5_C_re, s5_C_im, s5_D)
    h = x.reshape(n, d)
    vec = lambda a: a.reshape(1, d)
    nbd = 2 * N_HEADS
    w_in_p = jnp.concatenate(
        [mix_w_in[:, :, :4 * GROUP_WIDTH], mix_w_in[:, :, 4 * GROUP_WIDTH + nbd:],
         mix_w_in[:, :, 4 * GROUP_WIDTH:4 * GROUP_WIDTH + nbd],
         jnp.zeros((depth, d, IN_COLS_PAD - mix_w_in.shape[2]), mix_w_in.dtype)], axis=2).astype(BF16)
    wi_all = ffn_wi.astype(BF16)
    wo_all = ffn_wo.astype(BF16)
    for i in range(depth):
        h = _ffn(h, wi_all, wo_all, i, 0, vec(ln_g[i, 0]), vec(ln_b[i, 0]))
        h = _token_mix(h, i, bsz, t_len, w_in_p, mix_w_out[i].astype(BF16), gdn_conv_w[i], gdn_A_log[i],
                       gdn_dt_bias[i], gdn_norm_g[i], lower_bounds[i], hgrn_norm_g[i], s5_ops, s5_glu_w[i],
                       s5_glu_b[i], lru_conv_w[i], lru_conv_b[i], lru_wa[i], lru_ba[i], lru_wx[i], lru_bx[i],
                       lru_param[i], branch_norm_g[i], vec(ln_g[i, 1]), vec(ln_b[i, 1]), consts)
        h = _ffn(h, wi_all, wo_all, i, 1, vec(ln_g[i, 2]), vec(ln_b[i, 2]))
        h = _ple(h, p[i].reshape(n, p.shape[-1]), ple_w[i].astype(BF16), ple_gate_w[i].astype(BF16),
                 vec(ln_g[i, 3]), vec(ln_b[i, 3]))
    return h.reshape(bsz, t_len, d)
```

```python
import numpy as np
import jax
import jax.numpy as jnp
from jax import lax
from jax.experimental import pallas as pl
from jax.experimental.pallas import tpu as pltpu

F32 = jnp.float32
BF16 = jnp.bfloat16
HI = lax.Precision.HIGHEST

D_MODEL = 2048
DEPTH = 2
GROUP_WIDTH = D_MODEL // 4
HEAD_DIM = 128
N_HEADS = GROUP_WIDTH // HEAD_DIM
S5_CH = 16
S5_GROUPS = GROUP_WIDTH // S5_CH
S5_STATE = 64
LRU_BLOCKS = 8
LRU_C = 8.0
CONV_WIDTH = 4
D_FF = ((8 * D_MODEL // 3 + 255) // 256) * 256
PLE_DIM = 256
DN_ALPHA = (2 * DEPTH) ** 0.25
LN_EPS = 1e-5
RMS_EPS = 1e-6

SUBLANES = 8
LANES = 128
VMEM_LIMIT = 56 * 1024 * 1024

CHUNK = 64
MIX_BLOCK = 512
S5_L = 16
IN_COLS_PAD = 6144
COL_QKV, COL_Z, COL_HG, COL_S5, COL_LX, COL_LG, COL_BD = 0, 1536, 2048, 4096, 4608, 5120, 5632


def _params(sem):
    return pltpu.CompilerParams(dimension_semantics=sem, vmem_limit_bytes=VMEM_LIMIT)


def _sigmoid(x):
    return 1.0 / (1.0 + jnp.exp(-x))


def _silu(x):
    return x * _sigmoid(x)


def _softplus(x):
    return jnp.maximum(x, 0.0) + jnp.log(1.0 + jnp.exp(-jnp.abs(x)))


def _gelu(x):
    return 0.5 * x * (1.0 + jnp.tanh(0.7978845608028654 * (x + 0.044715 * x * x * x)))


def _layer_norm(y, g, b):
    mu = jnp.mean(y, axis=-1, keepdims=True)
    yc = y - mu
    var = jnp.mean(yc * yc, axis=-1, keepdims=True)
    return yc * lax.rsqrt(var + LN_EPS) * g + b


def _rms(y, g):
    return y * lax.rsqrt(jnp.mean(y * y, axis=-1, keepdims=True) + RMS_EPS) * g


def _dot(a, b, precision=None):
    return jnp.dot(a, b, preferred_element_type=F32, precision=precision)


def _dot_nt(a, b, precision=None):
    return lax.dot_general(a, b, (((1,), (1,)), ((), ())), preferred_element_type=F32, precision=precision)


def _dot_tn(a, b, precision=None):
    return lax.dot_general(a, b, (((0,), (0,)), ((), ())), preferred_element_type=F32, precision=precision)


def _bdot(a, b, ca, cb):
    return lax.dot_general(a, b, (((ca,), (cb,)), ((0,), (0,))), preferred_element_type=F32)


def _bf(x):
    return x.astype(BF16)


def _dot_split(m, x):
    x1 = x.astype(BF16)
    r1 = x - x1.astype(F32)
    x2 = r1.astype(BF16)
    x3 = (r1 - x2.astype(F32)).astype(BF16)
    return _dot(m, x3) + _dot(m, x2) + _dot(m, x1)


def _ffn_kernel(x_ref, wg_ref, wu_ref, wo_ref, g_ref, b_ref, o_ref, xb_ref, acc_ref):
    f = pl.program_id(1)

    @pl.when(f == 0)
    def _():
        xb_ref[...] = x_ref[...].astype(BF16)
        acc_ref[...] = jnp.zeros_like(acc_ref)

    xb = xb_ref[...]
    hg = _dot(xb, wg_ref[...])
    hu = _dot(xb, wu_ref[...])
    act = (_silu(hg) * hu).astype(BF16)
    acc_ref[...] += _dot(act, wo_ref[...])

    @pl.when(f == pl.num_programs(1) - 1)
    def _():
        y = DN_ALPHA * x_ref[...] + 0.5 * acc_ref[...]
        o_ref[...] = _layer_norm(y, g_ref[...], b_ref[...])


def _ffn(x, wi_all, wo_all, layer, sub, g, b, tm=512, tf=512):
    n, d = x.shape
    dff = wo_all.shape[2]
    tm = min(tm, n)
    nf = dff // tf
    return pl.pallas_call(
        _ffn_kernel,
        grid=(n // tm, nf),
        in_specs=[
            pl.BlockSpec((tm, d), lambda i, f: (i, 0)),
            pl.BlockSpec((None, None, d, tf), lambda i, f: (layer, sub, 0, f)),
            pl.BlockSpec((None, None, d, tf), lambda i, f: (layer, sub, 0, f + nf)),
            pl.BlockSpec((None, None, tf, d), lambda i, f: (layer, sub, f, 0)),
            pl.BlockSpec((1, d), lambda i, f: (0, 0)),
            pl.BlockSpec((1, d), lambda i, f: (0, 0)),
        ],
        out_specs=pl.BlockSpec((tm, d), lambda i, f: (i, 0)),
        out_shape=jax.ShapeDtypeStruct((n, d), F32),
        scratch_shapes=[pltpu.VMEM((tm, d), BF16), pltpu.VMEM((tm, d), F32)],
        compiler_params=_params(("parallel", "arbitrary")),
        name="ffn",
    )(x, wi_all, wi_all, wo_all, g, b)


def _proj_kernel(x_ref, w_ref, o_ref, xb_ref):
    @pl.when(pl.program_id(1) == 0)
    def _():
        xb_ref[...] = x_ref[...].astype(BF16)

    o_ref[...] = _dot(xb_ref[...], w_ref[...])


def _proj(x, w_all, layer, tm=1024, tn=1024):
    n, d = x.shape
    cols = w_all.shape[2]
    tm = min(tm, n)
    return pl.pallas_call(
        _proj_kernel,
        grid=(n // tm, cols // tn),
        in_specs=[pl.BlockSpec((tm, d), lambda i, j: (i, 0)),
                  pl.BlockSpec((None, d, tn), lambda i, j: (layer, 0, j))],
        out_specs=pl.BlockSpec((tm, tn), lambda i, j: (i, j)),
        out_shape=jax.ShapeDtypeStruct((n, cols), F32),
        scratch_shapes=[pltpu.VMEM((tm, d), BF16)],
        compiler_params=_params(("parallel", "arbitrary")),
        name="in_proj",
    )(x, w_all)


def _causal_conv(x, w_ref, xe_ref, first):
    tb = x.shape[0]

    @pl.when(first)
    def _():
        xe_ref[0:SUBLANES, :] = jnp.zeros((SUBLANES, x.shape[1]), F32)

    xe_ref[SUBLANES:SUBLANES + tb, :] = x
    acc = x * w_ref[CONV_WIDTH - 1:CONV_WIDTH, :]
    for j in range(CONV_WIDTH - 1):
        off = SUBLANES - (CONV_WIDTH - 1) + j
        acc = acc + xe_ref[off:off + tb, :] * w_ref[j:j + 1, :]
    xe_ref[0:SUBLANES, :] = xe_ref[tb:tb + SUBLANES, :]
    return acc


def _gdn_kernel(qkv_ref, z_ref, bd_ref, cw_ref, alog_ref, dtb_ref, ng_ref, tri_ref, o_ref, xe_ref, s_ref):
    tb = qkv_ref.shape[0]
    w = GROUP_WIDTH
    nch = tb // CHUNK
    first = pl.program_id(1) == 0

    @pl.when(first)
    def _():
        s_ref[...] = jnp.zeros_like(s_ref)

    qkv = _silu(_causal_conv(qkv_ref[...], cw_ref, xe_ref, first))
    bd = bd_ref[...]
    beta_all = _sigmoid(bd)
    g_all = -jnp.exp(alog_ref[...]) * _softplus(bd + dtb_ref[...])
    tri = tri_ref[...]

    def stack(col0):
        parts = []
        for ci in range(nch):
            for h in range(N_HEADS):
                parts.append(qkv[ci * CHUNK:(ci + 1) * CHUNK, col0 + h * HEAD_DIM:col0 + (h + 1) * HEAD_DIM][None])
        return jnp.concatenate(parts, axis=0)

    q, k, v = stack(0), stack(w), stack(2 * w)
    q = q * (lax.rsqrt(jnp.sum(q * q, axis=-1, keepdims=True) + RMS_EPS) * (HEAD_DIM ** -0.5))
    k = k * lax.rsqrt(jnp.sum(k * k, axis=-1, keepdims=True) + RMS_EPS)
    gcs, grs, bcs = [], [], []
    for ci in range(nch):
        rows = slice(ci * CHUNK, (ci + 1) * CHUNK)
        gcum = _dot_split(tri, g_all[rows, :])
        gcum_t = gcum.T
        for h in range(N_HEADS):
            gcs.append(gcum[:, N_HEADS + h:N_HEADS + h + 1][None])
            grs.append(gcum_t[N_HEADS + h:N_HEADS + h + 1, :][None])
            bcs.append(beta_all[rows, h:h + 1][None])
    gc = jnp.concatenate(gcs, axis=0)
    gr = jnp.concatenate(grs, axis=0)
    bc = jnp.concatenate(bcs, axis=0)

    r = lax.broadcasted_iota(jnp.int32, (1, CHUNK, CHUNK), 1)
    c = lax.broadcasted_iota(jnp.int32, (1, CHUNK, CHUNK), 2)
    eye = (r == c).astype(F32)
    dec = jnp.exp(jnp.where(r >= c, gc - gr, -jnp.inf))
    kb = _bf(k)
    a = bc * _bdot(kb, kb, 2, 2) * jnp.where(r > c, dec, 0.0)
    qk = _bdot(_bf(q), kb, 2, 2) * dec
    t = eye - a
    p = _bdot(_bf(a), _bf(a), 2, 1)
    n = 2
    while True:
        t = t + _bdot(_bf(t), _bf(p), 2, 1)
        n *= 2
        if n >= CHUNK:
            break
        p = _bdot(_bf(p), _bf(p), 2, 1)
    eg = jnp.exp(gc)
    rhs = jnp.concatenate([bc * v, (bc * eg) * k], axis=2)
    solb = _bf(rhs + _bdot(_bf(t - eye), _bf(rhs), 2, 1))
    glast = gc[:, CHUNK - 1:CHUNK, :]
    kd = k * jnp.exp(glast - gc)
    nk = _bdot(_bf(kd), solb, 1, 1)
    qw = _bdot(_bf(qk), solb, 2, 1)
    qeff = _bf(q * eg - qw[:, :, HEAD_DIM:])
    nku = nk[:, :, :HEAD_DIM]
    nkw = _bf(nk[:, :, HEAD_DIM:])
    gl = jnp.exp(glast)
    ng = ng_ref[...]

    s = s_ref[...]
    for ci in range(nch):
        sel = slice(ci * N_HEADS, (ci + 1) * N_HEADS)
        sb = _bf(s)
        o = qw[sel, :, :HEAD_DIM] + _bdot(qeff[sel], sb, 2, 1)
        s = gl[sel] * s + nku[sel] - _bdot(nkw[sel], sb, 2, 1)
        rows = slice(ci * CHUNK, (ci + 1) * CHUNK)
        for h in range(N_HEADS):
            hs = slice(h * HEAD_DIM, (h + 1) * HEAD_DIM)
            o_ref[rows, hs] = _rms(o[h], ng) * _silu(z_ref[rows, hs])
    s_ref[...] = s


def _gdn(y, bsz, t_len, conv_w, alog_row, dtb_row, norm_g, tri):
    tb = min(MIX_BLOCK, t_len)
    nb = t_len // tb
    w = GROUP_WIDTH
    row = lambda b, i: b * nb + i
    const = lambda b, i: (0, 0)
    return pl.pallas_call(
        _gdn_kernel,
        grid=(bsz, nb),
        in_specs=[
            pl.BlockSpec((tb, 3 * w), lambda b, i: (row(b, i), COL_QKV // (3 * w))),
            pl.BlockSpec((tb, w), lambda b, i: (row(b, i), COL_Z // w)),
            pl.BlockSpec((tb, LANES), lambda b, i: (row(b, i), COL_BD // LANES)),
            pl.BlockSpec((CONV_WIDTH, 3 * w), const),
            pl.BlockSpec((1, LANES), const),
            pl.BlockSpec((1, LANES), const),
            pl.BlockSpec((1, HEAD_DIM), const),
            pl.BlockSpec((CHUNK, CHUNK), const),
        ],
        out_specs=pl.BlockSpec((tb, w), lambda b, i: (row(b, i), 0)),
        out_shape=jax.ShapeDtypeStruct((bsz * t_len, w), F32),
        scratch_shapes=[pltpu.VMEM((tb + SUBLANES, 3 * w), F32),
                        pltpu.VMEM((N_HEADS, HEAD_DIM, HEAD_DIM), F32)],
        compiler_params=_params(("parallel", "arbitrary")),
        name="gdn",
    )(y, y, y, conv_w, alog_row, dtb_row, norm_g, tri)


def _hgrn_masks():
    c = CHUNK
    levels = int(np.log2(c))
    m = np.zeros((levels + 2, c, c), np.float32)
    pm = np.zeros((levels + 1, c, c), np.float32)
    for l in range(levels):
        n = c >> (l + 1)
        for row in range(c):
            base = (row // (2 * n)) * 2 * n
            mid = base + n
            if row >= mid:
                m[l, row, mid:row + 1] = 1.0
            else:
                m[l, row, row + 1:mid] = 1.0
        for t in range(c):
            for s in range(c):
                if t // (2 * n) == s // (2 * n) and t % (2 * n) >= n and s % (2 * n) < n:
                    pm[l, t, s] = 1.0
    pm[levels] = np.eye(c, dtype=np.float32)
    m[levels] = np.tril(np.ones((c, c), np.float32))
    m[levels + 1] = np.triu(np.ones((c, c), np.float32), 1)
    return m.reshape((levels + 2) * c, c), pm


def _hgrn_kernel(x_ref, lb_ref, ng_ref, m_ref, pm_ref, o_ref, st_ref):
    tb = x_ref.shape[0]
    w = GROUP_WIDTH
    nch = tb // CHUNK
    levels = pm_ref.shape[0] - 1

    @pl.when(pl.program_id(1) == 0)
    def _():
        st_ref[...] = jnp.zeros_like(st_ref)

    lb = lb_ref[...]
    ng = ng_ref[...]
    m_all = m_ref[...]
    fl = x_ref[:, w:2 * w]
    e = jnp.exp(-jnp.abs(fl))
    rcp = 1.0 / (1.0 + e)
    sig = jnp.where(fl >= 0, rcp, e * rcp)
    nsig = jnp.where(fl >= 0, e * rcp, rcp)
    log_f = jnp.log(lb + (1.0 - lb) * sig)
    kk = (1.0 - lb) * nsig
    e_all = [jnp.exp(_dot_split(m_all, log_f[ci * CHUNK:(ci + 1) * CHUNK, :])) for ci in range(nch)]

    def stack(get):
        return jnp.concatenate([get(ci, h)[None] for ci in range(nch) for h in range(N_HEADS)], axis=0)

    blk = lambda a, ci, h: a[ci * CHUNK:(ci + 1) * CHUNK, h * HEAD_DIM:(h + 1) * HEAD_DIM]
    lvl = lambda l: stack(lambda ci, h: e_all[ci][l * CHUNK:(l + 1) * CHUNK, h * HEAD_DIM:(h + 1) * HEAD_DIM])
    q = stack(lambda ci, h: x_ref[ci * CHUNK:(ci + 1) * CHUNK, h * HEAD_DIM:(h + 1) * HEAD_DIM]) * (HEAD_DIM ** -0.5)
    k = stack(lambda ci, h: blk(kk, ci, h))
    vb = _bf(stack(lambda ci, h: x_ref[ci * CHUNK:(ci + 1) * CHUNK, 2 * w + h * HEAD_DIM:2 * w + (h + 1) * HEAD_DIM]))
    att = _bdot(_bf(q), _bf(k), 2, 2) * pm_ref[levels][None]
    for l in range(levels):
        el = lvl(l)
        att = att + _bdot(_bf(q * el), _bf(k * el), 2, 2) * pm_ref[l][None]
    eb = lvl(levels)
    er = lvl(levels + 1)
    o_intra = _bdot(_bf(att), vb, 2, 1)
    upd = _bdot(vb, _bf(k * er), 1, 1)
    qe = _bf(q * eb)
    dlast = eb[:, CHUNK - 1:CHUNK, :]

    st = st_ref[...]
    for ci in range(nch):
        sel = slice(ci * N_HEADS, (ci + 1) * N_HEADS)
        o = o_intra[sel] + _bdot(qe[sel], _bf(st), 2, 2)
        st = st * dlast[sel] + upd[sel]
        rows = slice(ci * CHUNK, (ci + 1) * CHUNK)
        for h in range(N_HEADS):
            cs = slice(h * HEAD_DIM, (h + 1) * HEAD_DIM)
            gg = x_ref[rows, 3 * w + h * HEAD_DIM:3 * w + (h + 1) * HEAD_DIM]
            o_ref[rows, cs] = _rms(o[h], ng) * _silu(gg)
    st_ref[...] = st


def _hgrn(y, bsz, t_len, lb_row, norm_g, m_all, pm):
    tb = min(MIX_BLOCK, t_len)
    nb = t_len // tb
    w = GROUP_WIDTH
    const2 = lambda b, i: (0, 0)
    return pl.pallas_call(
        _hgrn_kernel,
        grid=(bsz, nb),
        in_specs=[
            pl.BlockSpec((tb, 4 * w), lambda b, i: (b * nb + i, COL_HG // (4 * w))),
            pl.BlockSpec((1, w), const2),
            pl.BlockSpec((1, HEAD_DIM), const2),
            pl.BlockSpec(m_all.shape, const2),
            pl.BlockSpec(pm.shape, lambda b, i: (0, 0, 0)),
        ],
        out_specs=pl.BlockSpec((tb, w), lambda b, i: (b * nb + i, 0)),
        out_shape=jax.ShapeDtypeStruct((bsz * t_len, w), F32),
        scratch_shapes=[pltpu.VMEM((N_HEADS, HEAD_DIM, HEAD_DIM), F32)],
        compiler_params=_params(("parallel", "arbitrary")),
        name="hgrn2",
    )(y, lb_row, norm_g, m_all, pm)


S5_PAIR = 2 * S5_CH
S5_PAIRS = S5_GROUPS // 2
S5_FEAT = S5_L * S5_PAIR
S5_NS = 2 * S5_STATE
S5_QUADS = LANES // S5_PAIR


S5_KP = 24


def _s5_prep_kernel(bre_ref, bim_ref, ctre_ref, ctim_ref, cre_ref, cim_ref, pw_ref, pwt_ref,
                    pzh_ref, pzl_ref, qeh_ref, qel_ref, tph_ref, tpl_ref):
    el = S5_L
    bre, bim = bre_ref[0], bim_ref[0]
    ctre, ctim = ctre_ref[0], ctim_ref[0]
    cre, cim = cre_ref[0], cim_ref[0]
    pw = pw_ref[0]
    pwt = pwt_ref[0]

    def put(hi_ref, lo_ref, rows, val):
        hi = val.astype(BF16)
        hi_ref[0, rows, :] = hi
        lo_ref[0, rows, :] = (val - hi.astype(F32)).astype(BF16)

    def b_times_power(k):
        lr_, li_ = pw[k:k + 1, :], pw[S5_KP + k:S5_KP + k + 1, :]
        return bre * lr_ - bim * li_, bre * li_ + bim * lr_

    kerns = []
    for tau in range(el):
        xr, xi = b_times_power(tau)
        kerns.append(_dot_nt(xr, ctre, HI) - _dot_nt(xi, ctim, HI))
    zero = jnp.zeros((S5_PAIR, S5_PAIR), F32)
    for s in range(el):
        put(tph_ref, tpl_ref, slice(s * S5_PAIR, (s + 1) * S5_PAIR),
            jnp.concatenate([zero] * s + kerns[:el - s], axis=1))
    for l in range(el):
        xr, xi = b_times_power(el - 1 - l)
        put(pzh_ref, pzl_ref, slice(l * S5_PAIR, (l + 1) * S5_PAIR), jnp.concatenate([xr, xi], axis=1))
    tops, bots = [], []
    for l in range(el):
        cr_, ci_ = pwt[:, l + 1:l + 2], pwt[:, S5_KP + l + 1:S5_KP + l + 2]
        tops.append(cre * cr_ - cim * ci_)
        bots.append(-(cre * ci_ + cim * cr_))
    put(qeh_ref, qel_ref, slice(0, S5_NS), jnp.concatenate(tops, axis=1))
    put(qeh_ref, qel_ref, slice(S5_NS, 2 * S5_NS), jnp.concatenate(bots, axis=1))


def _s5_operators(lam_re, lam_im, log_dt, b_re, b_im, c_re, c_im, d_skip):
    dep = lam_re.shape[0]
    n, p, el, gp = S5_STATE, S5_CH, S5_L, S5_PAIRS
    dt = jnp.exp(log_dt)[..., None]
    lr, li = lam_re * dt, lam_im * dt
    mag = jnp.exp(lr)
    ab_re, ab_im = mag * jnp.cos(li), mag * jnp.sin(li)
    den = lam_re * lam_re + lam_im * lam_im
    nr = ab_re - 1.0
    cc_re = (nr * lam_re + ab_im * lam_im) / den
    cc_im = (ab_im * lam_re - nr * lam_im) / den
    bb_re = cc_re[..., None] * b_re - cc_im[..., None] * b_im
    bb_im = cc_re[..., None] * b_im + cc_im[..., None] * b_re
    ks = jnp.arange(S5_KP, dtype=F32)[:, None, None, None]
    pw_re = jnp.exp(ks * lr[None]) * jnp.cos(ks * li[None])
    pw_im = jnp.exp(ks * lr[None]) * jnp.sin(ks * li[None])
    eye2 = jnp.eye(2, dtype=F32)
    npair = dep * gp
    bt = [jnp.einsum('dagnq,gh->dagqhn', t.reshape(dep, gp, 2, n, p), eye2).reshape(npair, S5_PAIR, S5_NS)
          for t in (bb_re, bb_im)]
    ct = [jnp.einsum('dahpn,gh->dahpgn', t.reshape(dep, gp, 2, p, n), eye2).reshape(npair, S5_PAIR, S5_NS)
          for t in (c_re, c_im)]
    cc = [jnp.swapaxes(t, 1, 2) for t in ct]
    pw = jnp.concatenate([jnp.moveaxis(t.reshape(S5_KP, dep, gp, 2, n), 0, 2).reshape(npair, S5_KP, S5_NS)
                          for t in (pw_re, pw_im)], axis=1)
    pwt = jnp.swapaxes(pw, 1, 2)
    blk = lambda a: pl.BlockSpec((1,) + a.shape[1:], lambda i: (i, 0, 0))
    outs = [jax.ShapeDtypeStruct((npair, S5_FEAT, 2 * S5_NS), BF16)] * 2 \
        + [jax.ShapeDtypeStruct((npair, 2 * S5_NS, S5_FEAT), BF16)] * 2 \
        + [jax.ShapeDtypeStruct((npair, S5_FEAT, S5_FEAT), BF16)] * 2
    args = bt + ct + cc + [pw, pwt]
    ops = pl.pallas_call(
        _s5_prep_kernel,
        grid=(npair,),
        in_specs=[blk(a) for a in args],
        out_specs=[blk(o) for o in outs],
        out_shape=outs,
        compiler_params=_params(("parallel",)),
        name="s5_prep",
    )(*args)
    ops = [o.reshape((dep, gp) + o.shape[1:]) for o in ops]
    cmul = lambda x, y: (x[0] * y[0] - x[1] * y[1], x[0] * y[1] + x[1] * y[0])
    a1 = (pw_re[el], pw_im[el])
    a2 = cmul(a1, a1)
    a4 = cmul(a2, a2)
    a3 = cmul(a2, a1)
    pows = [(jnp.ones_like(a1[0]), jnp.zeros_like(a1[0])), a1, a2, a3, a4, cmul(a4, a1), cmul(a4, a2), cmul(a4, a3)]
    a8 = cmul(a4, a4)
    rows = [t[0] for t in pows] + [t[1] for t in pows] + [a1[0], a1[1], a2[0], a2[1], a4[0], a4[1], a8[0], a8[1]]
    coef = jnp.stack([t.reshape(dep, gp, S5_NS) for t in rows], axis=2)
    d2 = jnp.tile(d_skip.reshape(dep, gp, 1, S5_PAIR), (1, 1, el, 1)).reshape(dep, gp, 1, S5_FEAT)
    return tuple(ops) + (coef, d2)


def _dot3(x, w_hi, w_lo):
    x_hi = x.astype(BF16)
    x_lo = (x - x_hi.astype(F32)).astype(BF16)
    return _dot(x_lo, w_hi) + _dot(x_hi, w_lo) + _dot(x_hi, w_hi)


def _s5_kernel(u_ref, pzh_ref, pzl_ref, qeh_ref, qel_ref, tph_ref, tpl_ref, cf_ref, d_ref, y_ref):
    nc = u_ref.shape[0]
    ns = S5_NS
    tok = [u_ref[:, l, :] for l in range(S5_L)]
    rowmod = lax.broadcasted_iota(jnp.int32, (nc, ns), 0) % SUBLANES
    ys = []
    for qd in range(S5_QUADS):
        lanes = slice(qd * S5_PAIR, (qd + 1) * S5_PAIR)
        u = jnp.concatenate([t[:, lanes] for t in tok], axis=1)
        cf = cf_ref[0, qd]
        z = _dot3(u, pzh_ref[0, qd], pzl_ref[0, qd])
        zr, zi = z[:, :ns], z[:, ns:]
        for j, dist in enumerate((1, 2, 4)):
            ar = cf[2 * SUBLANES + 2 * j:2 * SUBLANES + 2 * j + 1, :]
            ai = cf[2 * SUBLANES + 2 * j + 1:2 * SUBLANES + 2 * j + 2, :]
            keep = rowmod >= dist
            sr = jnp.where(keep, pltpu.roll(zr, dist, 0), 0.0)
            si = jnp.where(keep, pltpu.roll(zi, dist, 0), 0.0)
            zr, zi = zr + ar * sr - ai * si, zi + ar * si + ai * sr
        er = jnp.where(rowmod >= 1, pltpu.roll(zr, 1, 0), 0.0)
        ei = jnp.where(rowmod >= 1, pltpu.roll(zi, 1, 0), 0.0)
        pr, pi = cf[0:SUBLANES, :], cf[SUBLANES:2 * SUBLANES, :]
        a8r, a8i = cf[2 * SUBLANES + 6:2 * SUBLANES + 7, :], cf[2 * SUBLANES + 7:2 * SUBLANES + 8, :]
        xr = jnp.zeros((1, ns), F32)
        xi = jnp.zeros((1, ns), F32)
        ent_r, ent_i = [], []
        for t in range(nc // SUBLANES):
            rows = slice(t * SUBLANES, (t + 1) * SUBLANES)
            ent_r.append(er[rows, :] + pr * xr - pi * xi)
            ent_i.append(ei[rows, :] + pr * xi + pi * xr)
            lr_, li_ = zr[(t + 1) * SUBLANES - 1:(t + 1) * SUBLANES, :], zi[(t + 1) * SUBLANES - 1:(t + 1) * SUBLANES, :]
            xr, xi = lr_ + a8r * xr - a8i * xi, li_ + a8r * xi + a8i * xr
        ent = jnp.concatenate([jnp.concatenate(ent_r, axis=0), jnp.concatenate(ent_i, axis=0)], axis=1)
        y = _dot3(u, tph_ref[0, qd], tpl_ref[0, qd]) + _dot3(ent, qeh_ref[0, qd], qel_ref[0, qd]) + d_ref[0, qd] * u
        ys.append(_gelu(y))
    for l in range(S5_L):
        y_ref[:, l, :] = jnp.concatenate([y[:, l * S5_PAIR:(l + 1) * S5_PAIR] for y in ys], axis=1)


def _s5(y3, ops, layer, bsz):
    rows = y3.shape[0]
    nc = rows // bsz
    pzh, pzl, qeh, qel, tph, tpl, coef, d2 = ops
    nblk = GROUP_WIDTH // LANES
    op = lambda a: pl.BlockSpec((1, S5_QUADS) + a.shape[2:], lambda s, b: (layer, s, 0, 0))
    return pl.pallas_call(
        _s5_kernel,
        grid=(nblk, bsz),
        in_specs=[pl.BlockSpec((nc, S5_L, LANES), lambda s, b: (b, 0, COL_S5 // LANES + s)),
                  op(pzh), op(pzl), op(qeh), op(qel), op(tph), op(tpl), op(coef), op(d2)],
        out_specs=pl.BlockSpec((nc, S5_L, LANES), lambda s, b: (b, 0, s)),
        out_shape=jax.ShapeDtypeStruct((rows, S5_L, GROUP_WIDTH), F32),
        compiler_params=_params(("parallel", "parallel")),
        name="s5",
    )(y3, pzh, pzl, qeh, qel, tph, tpl, coef, d2)


def _lru_kernel(x_ref, gate_ref, cw_ref, cb_ref, wa_ref, ba_ref, wx_ref, bx_ref, lp_ref, ng_ref, o_ref,
                xe_ref, a_s, v_s, h_ref):
    tb = x_ref.shape[0]
    first = pl.program_id(1) == 0

    @pl.when(first)
    def _():
        h_ref[...] = jnp.zeros_like(h_ref)

    xc = _causal_conv(x_ref[...], cw_ref, xe_ref, first) + cb_ref[...]
    r = _sigmoid(_dot(xc, wa_ref[...]) + ba_ref[...])
    gi = _sigmoid(_dot(xc, wx_ref[...]) + bx_ref[...])
    log_a = -LRU_C * r * _softplus(-lp_ref[...])
    a = jnp.exp(log_a)
    th = jnp.tanh(log_a)
    v = jnp.sqrt(-2.0 * th / (1.0 - th)) * gi * xc
    rowmod = lax.broadcasted_iota(jnp.int32, a.shape, 0) % SUBLANES
    for dist in (1, 2, 4):
        keep = rowmod >= dist
        a_sh = jnp.where(keep, pltpu.roll(a, dist, 0), 1.0)
        v_sh = jnp.where(keep, pltpu.roll(v, dist, 0), 0.0)
        v = v + a * v_sh
        a = a * a_sh
    a_s[...] = a
    v_s[...] = v
    h = h_ref[...]
    for t in range(tb // SUBLANES):
        rows = slice(t * SUBLANES, (t + 1) * SUBLANES)
        ht = v_s[rows, :] + a_s[rows, :] * h
        v_s[rows, :] = ht
        h = ht[SUBLANES - 1:SUBLANES, :]
    h_ref[...] = h
    o_ref[...] = _rms(v_s[...] * _gelu(gate_ref[...]), ng_ref[...])


def _lru(y, bsz, t_len, conv_w, conv_b, wa_bd, ba, wx_bd, bx, lparam, norm_g):
    tb = min(MIX_BLOCK, t_len)
    nb = t_len // tb
    w = GROUP_WIDTH
    const = lambda b, i: (0, 0)
    vec = pl.BlockSpec((1, w), const)
    return pl.pallas_call(
        _lru_kernel,
        grid=(bsz, nb),
        in_specs=[
            pl.BlockSpec((tb, w), lambda b, i: (b * nb + i, COL_LX // w)),
            pl.BlockSpec((tb, w), lambda b, i: (b * nb + i, COL_LG // w)),
            pl.BlockSpec((CONV_WIDTH, w), const), vec,
            pl.BlockSpec((w, w), const), vec,
            pl.BlockSpec((w, w), const), vec, vec, vec,
        ],
        out_specs=pl.BlockSpec((tb, w), lambda b, i: (b * nb + i, 0)),
        out_shape=jax.ShapeDtypeStruct((bsz * t_len, w), F32),
        scratch_shapes=[pltpu.VMEM((tb + SUBLANES, w), F32), pltpu.VMEM((tb, w), F32),
                        pltpu.VMEM((tb, w), F32), pltpu.VMEM((1, w), F32)],
        compiler_params=_params(("parallel", "arbitrary")),
        name="rglru",
    )(y, y, conv_w, conv_b, wa_bd, ba, wx_bd, bx, lparam, norm_g)


def _mix_out_kernel(x_ref, ya_ref, yb_ref, yc_ref, yd_ref, gw_ref, gb_ref, cg_ref, wo_ref, g_ref, b_ref, o_ref):
    w = GROUP_WIDTH
    yc = yc_ref[...]
    yc = yc * _sigmoid(_dot(yc, gw_ref[...]) + gb_ref[...])
    yc = _rms(yc, cg_ref[...])
    acc = _dot(ya_ref[...].astype(BF16), wo_ref[0:w, :])
    acc += _dot(yb_ref[...].astype(BF16), wo_ref[w:2 * w, :])
    acc += _dot(yc.astype(BF16), wo_ref[2 * w:3 * w, :])
    acc += _dot(yd_ref[...].astype(BF16), wo_ref[3 * w:4 * w, :])
    o_ref[...] = _layer_norm(DN_ALPHA * x_ref[...] + acc, g_ref[...], b_ref[...])


def _mix_out(x, ya, yb, yc, yd, glu_w, glu_b, c_norm_g, w_out, g, b, tm=512):
    n, d = x.shape
    w = GROUP_WIDTH
    tm = min(tm, n)
    const = lambda i: (0, 0)
    tok = lambda width: pl.BlockSpec((tm, width), lambda i: (i, 0))
    return pl.pallas_call(
        _mix_out_kernel,
        grid=(n // tm,),
        in_specs=[tok(d), tok(w), tok(w), tok(w), tok(w),
                  pl.BlockSpec((w, w), const), pl.BlockSpec((1, w), const), pl.BlockSpec((1, w), const),
                  pl.BlockSpec((d, d), const), pl.BlockSpec((1, d), const), pl.BlockSpec((1, d), const)],
        out_specs=tok(d),
        out_shape=jax.ShapeDtypeStruct((n, d), F32),
        compiler_params=_params(("parallel",)),
        name="mix_out",
    )(x, ya, yb, yc, yd, glu_w, glu_b, c_norm_g, w_out, g, b)


def _ple_kernel(x_ref, p_ref, wp_ref, wg_ref, g_ref, b_ref, o_ref):
    x = x_ref[...]
    gate = _sigmoid(_dot(x.astype(BF16), wg_ref[...]))
    ple = _dot(p_ref[...].astype(BF16), wp_ref[...]) * gate
    o_ref[...] = _layer_norm(DN_ALPHA * x + ple, g_ref[...], b_ref[...])


def _ple(x, p, wp, wg, g, b, tm=512):
    n, d = x.shape
    tm = min(tm, n)
    const = lambda i: (0, 0)
    return pl.pallas_call(
        _ple_kernel,
        grid=(n // tm,),
        in_specs=[pl.BlockSpec((tm, d), lambda i: (i, 0)), pl.BlockSpec((tm, p.shape[1]), lambda i: (i, 0)),
                  pl.BlockSpec(wp.shape, const), pl.BlockSpec(wg.shape, const),
                  pl.BlockSpec((1, d), const), pl.BlockSpec((1, d), const)],
        out_specs=pl.BlockSpec((tm, d), lambda i: (i, 0)),
        out_shape=jax.ShapeDtypeStruct((n, d), F32),
        compiler_params=_params(("parallel",)),
        name="ple",
    )(x, p, wp, wg, g, b)


def _block_diag(wb):
    nb, d, _ = wb.shape
    return jnp.einsum('nde,nm->ndme', wb, jnp.eye(nb, dtype=wb.dtype)).reshape(nb * d, nb * d)


def _pad_row(vals, offset):
    return jnp.zeros((1, LANES), F32).at[0, offset:offset + vals.shape[0]].set(vals)


def _token_mix(h, layer, bsz, t_len, w_in_p, w_out, gdn_conv_w, gdn_A_log, gdn_dt_bias, gdn_norm_g, lower_bound,
               hgrn_norm_g, s5_ops, s5_glu_w, s5_glu_b, lru_conv_w, lru_conv_b, lru_wa, lru_ba, lru_wx, lru_bx,
               lru_param, branch_norm_g, ln_g, ln_b, consts):
    w = GROUP_WIDTH
    n = bsz * t_len
    tri, m_all, pm = consts
    y = _proj(h, w_in_p, layer)
    ya = _gdn(y, bsz, t_len, gdn_conv_w, _pad_row(gdn_A_log, N_HEADS), _pad_row(gdn_dt_bias, N_HEADS),
              gdn_norm_g.reshape(1, HEAD_DIM), tri)
    yb = _hgrn(y, bsz, t_len, lower_bound.reshape(1, w), hgrn_norm_g.reshape(1, HEAD_DIM), m_all, pm)
    yc = _s5(y.reshape(n // S5_L, S5_L, y.shape[1]), s5_ops, layer, bsz).reshape(n, w)
    yd = _lru(y, bsz, t_len, lru_conv_w, lru_conv_b.reshape(1, w), _block_diag(lru_wa), lru_ba.reshape(1, w),
              _block_diag(lru_wx), lru_bx.reshape(1, w), lru_param.reshape(1, w), branch_norm_g[1].reshape(1, w))
    return _mix_out(h, ya, yb, yc, yd, s5_glu_w, s5_glu_b.reshape(1, w), branch_norm_g[0].reshape(1, w),
                    w_out, ln_g, ln_b)


def kernel(x, p, ln_g, ln_b, ffn_wi, ffn_wo, mix_w_in, mix_w_out, gdn_conv_w, gdn_A_log, gdn_dt_bias, gdn_norm_g, hgrn_lb_logits, hgrn_norm_g, s5_lam_re, s5_lam_im, s5_log_dt, s5_B_re, s5_B_im, s5_C_re, s5_C_im, s5_D, s5_glu_w, s5_glu_b, lru_conv_w, lru_conv_b, lru_wa, lru_ba, lru_wx, lru_bx, lru_param, branch_norm_g, ple_w, ple_gate_w):
    bsz, t_len, d = x.shape
    n = bsz * t_len
    depth = ln_g.shape[0]
    lb_cum = jnp.cumsum(jax.nn.softmax(hgrn_lb_logits.astype(F32), axis=0), axis=0)
    lower_bounds = lb_cum - lb_cum[0:1]
    m_all, pm = _hgrn_masks()
    consts = (jnp.tril(jnp.ones((CHUNK, CHUNK), BF16)), jnp.asarray(m_all, BF16), jnp.asarray(pm))
    s5_ops = _s5_operators(s5_lam_re, s5_lam_im, s5_log_dt, s5_B_re, s5_B_im, s5_C_re, s5_C_im, s5_D)
    h = x.reshape(n, d)
    vec = lambda a: a.reshape(1, d)
    nbd = 2 * N_HEADS
    w_in_p = jnp.concatenate(
        [mix_w_in[:, :, :4 * GROUP_WIDTH], mix_w_in[:, :, 4 * GROUP_WIDTH + nbd:],
         mix_w_in[:, :, 4 * GROUP_WIDTH:4 * GROUP_WIDTH + nbd],
         jnp.zeros((depth, d, IN_COLS_PAD - mix_w_in.shape[2]), mix_w_in.dtype)], axis=2).astype(BF16)
    wi_all = ffn_wi.astype(BF16)
    wo_all = ffn_wo.astype(BF16)
    for i in range(depth):
        h = _ffn(h, wi_all, wo_all, i, 0, vec(ln_g[i, 0]), vec(ln_b[i, 0]))
        h = _token_mix(h, i, bsz, t_len, w_in_p, mix_w_out[i].astype(BF16), gdn_conv_w[i], gdn_A_log[i],
                       gdn_dt_bias[i], gdn_norm_g[i], lower_bounds[i], hgrn_norm_g[i], s5_ops, s5_glu_w[i],
                       s5_glu_b[i], lru_conv_w[i], lru_conv_b[i], lru_wa[i], lru_ba[i], lru_wx[i], lru_bx[i],
                       lru_param[i], branch_norm_g[i], vec(ln_g[i, 1]), vec(ln_b[i, 1]), consts)
        h = _ffn(h, wi_all, wo_all, i, 1, vec(ln_g[i, 2]), vec(ln_b[i, 2]))
        h = _ple(h, p[i].reshape(n, p.shape[-1]), ple_w[i].astype(BF16), ple_gate_w[i].astype(BF16),
                 vec(ln_g[i, 3]), vec(ln_b[i, 3]))
    return h.reshape(bsz, t_len, d)
```
